```python
import math
import jax, jax.numpy as jnp
from jax import lax
import numpy as np

D_MODEL = 2048
BATCH = 4
SEQ = 2048
DEPTH = 4

HEAD_DIM = D_MODEL // 16
N_MEM = 256
MEM_HEADS = 4
DIL_GROUPS = ((128, 1), (512, 4), (2048, 16))
A_HEADS_PER_GROUP = D_MODEL // 256
A_BLOCK = 128
B_HEADS = 12
B_KV_GROUPS = 4
CMP_LEN = 32
CMP_STRIDE = 16
CMP_HIDDEN = 512
SLC_LEN = 64
SLC_TOPK = 16
SLC_Q_CHUNK = 16
SEL_FORCE = 1e4
WIN_LEN = 512
WIN_BLOCK = 128
D_FF = ((8 * D_MODEL // 3 + 255) // 256) * 256

kernel_name = "hybrid_dilated_nsa_yoco"


def rmsnorm(x, g, eps=1e-6):
    xf = x.astype(jnp.float32)
    y = xf * lax.rsqrt(jnp.mean(xf * xf, axis=-1, keepdims=True) + eps)
    return (y * g.astype(jnp.float32)).astype(x.dtype)


def alibi_slopes(n):
    return 2.0 ** (-8.0 * jnp.arange(1, n + 1, dtype=jnp.float32) / n)


def masked_softmax(s, valid):
    s = jnp.where(valid, s, -jnp.inf)
    m = jnp.max(s, axis=-1, keepdims=True)
    m = jnp.where(jnp.isfinite(m), m, 0.0)
    e = jnp.where(valid, jnp.exp(s - m), 0.0)
    den = jnp.maximum(jnp.sum(e, axis=-1, keepdims=True), 1e-30)
    return e / den, (m + jnp.log(den))[..., 0]


def banded_attention(q, k, v, slopes, max_dist, blk, pos_scale):
    B, L, H, D = q.shape
    G = k.shape[2]
    r = H // G
    nb = -(-L // blk)
    Lp = nb * blk
    n_prev = -(-max_dist // blk)
    qb = jnp.pad(q, ((0, 0), (0, Lp - L), (0, 0), (0, 0))).reshape(B, nb, blk, G, r, D)

    def band(t):
        t = jnp.pad(t, ((0, 0), (n_prev * blk, Lp - L), (0, 0), (0, 0)))
        t = t.reshape(B, nb + n_prev, blk, G, D)
        return jnp.concatenate([t[:, j:j + nb] for j in range(n_prev + 1)], axis=2)

    kb, vb = band(k), band(v)
    s = jnp.einsum('bnqgrd,bnkgd->bngrqk', qb, kb,
                   preferred_element_type=jnp.float32) / math.sqrt(D)
    qpos = jnp.arange(nb)[:, None] * blk + jnp.arange(blk)[None, :]
    kpos = jnp.arange(nb)[:, None] * blk + jnp.arange((n_prev + 1) * blk)[None, :] - n_prev * blk
    dist = qpos[:, :, None] - kpos[:, None, :]
    valid = (dist >= 0) & (dist <= max_dist) & (kpos[:, None, :] >= 0)
    bias = -slopes.reshape(1, G, r, 1, 1) * (dist * pos_scale).astype(jnp.float32)[:, None, None]
    p, lse = masked_softmax(s + bias, valid[:, None, None])
    o = jnp.einsum('bngrqk,bnkgd->bnqgrd', p.astype(v.dtype), vb).reshape(B, Lp, H, D)[:, :L]
    lse = lse.transpose(0, 1, 4, 2, 3).reshape(B, Lp, H)[:, :L]
    return o, lse


def dilated_attention(q, k, v):
    B, S, _, D = q.shape
    Hg = A_HEADS_PER_GROUP
    slopes = alibi_slopes(len(DIL_GROUPS) * Hg)
    outs, lses = [], []
    for gi, (w, d) in enumerate(DIL_GROUPS):
        sl = slice(gi * Hg, (gi + 1) * Hg)

        def to_cls(t):
            return t.reshape(B, S // d, d, Hg, D).transpose(0, 2, 1, 3, 4).reshape(B * d, S // d, Hg, D)

        o, lse = banded_attention(to_cls(q[:, :, sl]), to_cls(k[:, :, sl]), to_cls(v[:, :, sl]),
                                  slopes[sl], w // d, A_BLOCK, d)
        outs.append(o.reshape(B, d, S // d, Hg, D).transpose(0, 2, 1, 3, 4).reshape(B, S, Hg, D))
        lses.append(lse.reshape(B, d, S // d, Hg).transpose(0, 2, 1, 3).reshape(B, S, Hg))
    alpha = jax.nn.softmax(jnp.stack(lses, 0), axis=0)
    o = jnp.sum(alpha[..., None] * jnp.stack(outs, 0).astype(jnp.float32), axis=0)
    return o.astype(q.dtype)


def memory_attention(q, mem_n, w_kv):
    B, M, _ = mem_n.shape
    kv = (mem_n @ w_kv).reshape(B, M, 2, MEM_HEADS, HEAD_DIM)
    s = jnp.einsum('bshd,bmhd->bhsm', q, kv[:, :, 0],
                   preferred_element_type=jnp.float32) / math.sqrt(HEAD_DIM)
    p = jax.nn.softmax(s, axis=-1)
    return jnp.einsum('bhsm,bmhd->bshd', p.astype(q.dtype), kv[:, :, 1])


def compress_blocks(kr, pe, w1, w2):
    B, S, G, D = kr.shape
    n = (S - CMP_LEN) // CMP_STRIDE + 1
    idx = jnp.arange(n)[:, None] * CMP_STRIDE + jnp.arange(CMP_LEN)[None, :]
    blk = kr[:, idx] + pe[:, None, :].astype(kr.dtype)
    blk = blk.transpose(0, 1, 3, 2, 4).reshape(B, n, G, CMP_LEN * D)
    return jax.nn.gelu(blk @ w1) @ w2


def nsa_shared_kv(h, g, w, pe, wk1, wk2, wv1, wv2):
    B, S, _ = h.shape
    kv = (rmsnorm(h, g) @ w).reshape(B, S, 6, B_KV_GROUPS, HEAD_DIM)
    kc = compress_blocks(kv[:, :, 0], pe[0], wk1, wk2)
    vc = compress_blocks(kv[:, :, 1], pe[1], wv1, wv2)
    return kc, vc, kv[:, :, 2], kv[:, :, 3], kv[:, :, 4], kv[:, :, 5]


def selected_attention(qg, ks, vs, sel, slopes):
    B, S, G, r, D = qg.shape
    kk = sel.shape[-1]
    n_sel = S // SLC_LEN
    kb = ks.reshape(B, n_sel, SLC_LEN, G, D).transpose(0, 3, 1, 2, 4)
    vb = vs.reshape(B, n_sel, SLC_LEN, G, D).transpose(0, 3, 1, 2, 4)
    C = min(SLC_Q_CHUNK, S)
    nC = S // C
    q_ch = qg.reshape(B, nC, C, G, r, D).transpose(1, 0, 2, 3, 4, 5)
    s_ch = sel.reshape(B, nC, C, G, kk).transpose(1, 0, 2, 3, 4)
    t_ch = jnp.arange(S).reshape(nC, C)
    bi = jnp.arange(B)[:, None, None, None]
    gi = jnp.arange(G)[None, None, :, None]
    sl = slopes.reshape(G, r)

    def one(args):
        qc, sc, tc = args
        kg = kb[bi, gi, sc]
        vg = vb[bi, gi, sc]
        s = jnp.einsum('bcgrd,bcgkld->bcgrkl', qc, kg,
                       preferred_element_type=jnp.float32) / math.sqrt(D)
        kpos = sc[..., None] * SLC_LEN + jnp.arange(SLC_LEN)
        dist = tc[None, :, None, None, None] - kpos
        bias = -sl[None, None, :, :, None, None] * dist[:, :, :, None].astype(jnp.float32)
        p, _ = masked_softmax((s + bias).reshape(B, C, G, r, kk * SLC_LEN),
                              (dist >= 0).reshape(B, C, G, 1, kk * SLC_LEN))
        p = p.reshape(B, C, G, r, kk, SLC_LEN)
        return jnp.einsum('bcgrkl,bcgkld->bcgrd', p.astype(vg.dtype), vg)

    o = lax.map(one, (q_ch, s_ch, t_ch))
    return o.transpose(1, 0, 2, 3, 4, 5).reshape(B, S, G * r, D)


def nsa_attention(q, gates, shared):
    kc, vc, ks, vs, kw, vw = shared
    B, S, H, D = q.shape
    G = B_KV_GROUPS
    r = H // G
    slopes = alibi_slopes(H)
    sl = slopes.reshape(G, r)
    qg = q.reshape(B, S, G, r, D)
    t = jnp.arange(S)
    n_cmp = kc.shape[1]
    cend = jnp.arange(n_cmp) * CMP_STRIDE + CMP_LEN - 1
    dist_c = t[:, None] - cend[None, :]
    s = jnp.einsum('bsgrd,bngd->bsgrn', qg, kc,
                   preferred_element_type=jnp.float32) / math.sqrt(D)
    bias = -sl[None, :, :, None] * dist_c[:, None, None, :].astype(jnp.float32)
    p_c, _ = masked_softmax(s + bias, (dist_c >= 0)[:, None, None, :])
    o_cmp = jnp.einsum('bsgrn,bngd->bsgrd', p_c.astype(vc.dtype), vc).reshape(B, S, H, D)
    n_sel = S // SLC_LEN
    cs = jnp.arange(n_cmp) * CMP_STRIDE
    ss = jnp.arange(n_sel) * SLC_LEN
    ov = jnp.clip(jnp.minimum(cs[:, None] + CMP_LEN, ss[None, :] + SLC_LEN)
                  - jnp.maximum(cs[:, None], ss[None, :]), 0).astype(jnp.float32) / CMP_LEN
    imp = jnp.einsum('bsgrn,nj->bsgj', p_c, ov)
    jb = jnp.arange(n_sel)[None, :]
    cur = (t // SLC_LEN)[:, None]
    forced = (jb == 0) | (jb == cur) | (jb == cur - 1)
    imp = jnp.where(forced[:, None], SEL_FORCE, jnp.where((jb > cur)[:, None], -SEL_FORCE, imp))
    _, sel = lax.top_k(imp, min(SLC_TOPK, n_sel))
    o_slc = selected_attention(qg, ks, vs, sel, slopes)
    o_win, _ = banded_attention(q, kw, vw, slopes, WIN_LEN - 1, WIN_BLOCK, 1)
    o = (gates[..., 0:1] * o_cmp.astype(jnp.float32) + gates[..., 1:2] * o_slc.astype(jnp.float32)
         + gates[..., 2:3] * o_win.astype(jnp.float32))
    return o.astype(q.dtype)


def mixer_a(u, mem_n, w_in, w_mem_kv, w_out):
    B, S, _ = u.shape
    na = len(DIL_GROUPS) * A_HEADS_PER_GROUP
    proj = u @ w_in
    qkv = proj[..., :3 * na * HEAD_DIM].reshape(B, S, 3, na, HEAD_DIM)
    mq = proj[..., 3 * na * HEAD_DIM:].reshape(B, S, MEM_HEADS, HEAD_DIM)
    o_dil = dilated_attention(qkv[:, :, 0], qkv[:, :, 1], qkv[:, :, 2])
    o_mem = memory_attention(mq, mem_n, w_mem_kv)
    return jnp.concatenate([o_dil.reshape(B, S, -1), o_mem.reshape(B, S, -1)], axis=-1) @ w_out


def mixer_b(u, mem_n, shared, w_in, w_mem_kv, w_out):
    B, S, _ = u.shape
    qd = B_HEADS * HEAD_DIM
    proj = u @ w_in
    q = proj[..., :qd].reshape(B, S, B_HEADS, HEAD_DIM)
    gates = jax.nn.sigmoid(proj[..., qd:qd + 3 * B_HEADS].astype(jnp.float32)).reshape(B, S, B_HEADS, 3)
    mq = proj[..., qd + 3 * B_HEADS:].reshape(B, S, MEM_HEADS, HEAD_DIM)
    o_nsa = nsa_attention(q, gates, shared)
    o_mem = memory_attention(mq, mem_n, w_mem_kv)
    return jnp.concatenate([o_nsa.reshape(B, S, -1), o_mem.reshape(B, S, -1)], axis=-1) @ w_out


def swiglu(u, w_gu, w_down):
    gu = u @ w_gu
    return (jax.nn.silu(gu[..., :D_FF]) * gu[..., D_FF:]) @ w_down


def setup_inputs(seed: int = 0) -> dict:
    key = jax.random.key(seed)
    ks = jax.random.split(key, 17)
    n_a = DEPTH // 2
    n_b = DEPTH - n_a
    na = len(DIL_GROUPS) * A_HEADS_PER_GROUP
    a_cols = 3 * na * HEAD_DIM + MEM_HEADS * HEAD_DIM
    b_cols = B_HEADS * HEAD_DIM + 3 * B_HEADS + MEM_HEADS * HEAD_DIM
    a_out_in = A_HEADS_PER_GROUP * HEAD_DIM + MEM_HEADS * HEAD_DIM
    b_out_in = B_HEADS * HEAD_DIM + MEM_HEADS * HEAD_DIM
    f = jnp.float32

    def w(k, shape, fan_in):
        return jax.random.normal(k, shape, f) * fan_in ** -0.5

    return {
        "x": jax.random.normal(ks[0], (BATCH, SEQ, D_MODEL), f),
        "mem": jax.random.normal(ks[1], (BATCH, N_MEM, D_MODEL), f),
        "norm_g": 1.0 + 0.01 * jax.random.normal(ks[2], (DEPTH, 5, D_MODEL), f),
        "a_w_in": w(ks[3], (n_a, D_MODEL, a_cols), D_MODEL),
        "a_w_out": w(ks[4], (n_a, a_out_in, D_MODEL), a_out_in),
        "b_w_in": w(ks[5], (n_b, D_MODEL, b_cols), D_MODEL),
        "b_w_out": w(ks[6], (n_b, b_out_in, D_MODEL), b_out_in),
        "mem_w_kv": w(ks[7], (DEPTH, D_MODEL, 2 * MEM_HEADS * HEAD_DIM), D_MODEL),
        "ffn_w_gu": w(ks[8], (DEPTH, D_MODEL, 2 * D_FF), D_MODEL),
        "ffn_w_down": w(ks[9], (DEPTH, D_FF, D_MODEL), D_FF),
        "kv_norm_g": 1.0 + 0.01 * jax.random.normal(ks[10], (D_MODEL,), f),
        "kv_w": w(ks[11], (D_MODEL, 6 * B_KV_GROUPS * HEAD_DIM), D_MODEL),
        "cmp_pe": 0.02 * jax.random.normal(ks[12], (2, CMP_LEN, HEAD_DIM), f),
        "cmp_wk1": w(ks[13], (CMP_LEN * HEAD_DIM, CMP_HIDDEN), CMP_LEN * HEAD_DIM),
        "cmp_wk2": w(ks[14], (CMP_HIDDEN, HEAD_DIM), CMP_HIDDEN),
        "cmp_wv1": w(ks[15], (CMP_LEN * HEAD_DIM, CMP_HIDDEN), CMP_LEN * HEAD_DIM),
        "cmp_wv2": w(ks[16], (CMP_HIDDEN, HEAD_DIM), CMP_HIDDEN),
    }


def reference(x, mem, norm_g, a_w_in, a_w_out, b_w_in, b_w_out, mem_w_kv, ffn_w_gu,
              ffn_w_down, kv_norm_g, kv_w, cmp_pe, cmp_wk1, cmp_wk2, cmp_wv1, cmp_wv2):
    n_a = DEPTH // 2
    shared = None
    for l in range(DEPTH):
        g = norm_g[l]
        mem_n = rmsnorm(mem, g[4])
        u = rmsnorm(x, g[0])
        if l < n_a:
            o = mixer_a(u, mem_n, a_w_in[l], mem_w_kv[l], a_w_out[l])
        else:
            if l == n_a:
                shared = nsa_shared_kv(x, kv_norm_g, kv_w, cmp_pe, cmp_wk1, cmp_wk2, cmp_wv1, cmp_wv2)
            o = mixer_b(u, mem_n, shared, b_w_in[l - n_a], mem_w_kv[l], b_w_out[l - n_a])
        x = x + rmsnorm(o, g[1])
        x = x + rmsnorm(swiglu(rmsnorm(x, g[2]), ffn_w_gu[l], ffn_w_down[l]), g[3])
    return x
```

```python
import functools
import math

import jax
import jax.numpy as jnp
from jax import lax
from jax.experimental import pallas as pl
from jax.experimental.pallas import tpu as pltpu

BF = jnp.bfloat16
F32 = jnp.float32

D_MODEL = 2048
DEPTH = 4
HEAD_DIM = 128
N_MEM = 256
MEM_HEADS = 4
DIL_GROUPS = ((128, 1), (512, 4), (2048, 16))
A_HEADS_PER_GROUP = 8
A_BLOCK = 128
B_HEADS = 12
B_KV_GROUPS = 4
B_REP = B_HEADS // B_KV_GROUPS
CMP_LEN = 32
CMP_STRIDE = 16
CMP_HIDDEN = 512
SLC_LEN = 64
SLC_TOPK = 16
SEL_FORCE = 1e4
WIN_LEN = 512
D_FF = 5632
EPS = 1e-6

LANES = 128
VMEM_LIMIT = 56 * 1024 * 1024
NEG = -1e30
SCALE = 1.0 / math.sqrt(HEAD_DIM)
NT = (((1,), (1,)), ((), ()))


def _params(*sem):
    return pltpu.CompilerParams(dimension_semantics=sem, vmem_limit_bytes=VMEM_LIMIT)


def _masked_softmax(s, valid):
    s = jnp.where(valid, s, NEG)
    m = jnp.max(s, axis=-1, keepdims=True)
    e = jnp.where(valid, jnp.exp(s - m), 0.0)
    den = jnp.maximum(jnp.sum(e, axis=-1, keepdims=True), 1e-30)
    return e * (1.0 / den), m + jnp.log(den)


def _rms(x, g):
    return x * lax.rsqrt(jnp.mean(x * x, axis=-1, keepdims=True) + EPS) * g


def _rmsnorm_kernel(x_ref, g_ref, o_ref):
    o_ref[...] = _rms(x_ref[...], g_ref[...]).astype(o_ref.dtype)


def rmsnorm_rows(x, g, tm=512):
    M, D = x.shape
    tm = min(tm, M)
    return pl.pallas_call(
        _rmsnorm_kernel,
        grid=(M // tm,),
        in_specs=[pl.BlockSpec((tm, D), lambda i: (i, 0)),
                  pl.BlockSpec((1, D), lambda i: (0, 0))],
        out_specs=pl.BlockSpec((tm, D), lambda i: (i, 0)),
        out_shape=jax.ShapeDtypeStruct((M, D), BF),
        compiler_params=_params("arbitrary"),
        name="rmsnorm",
    )(x, g.reshape(1, D))


def _mm_kernel(x_ref, w_ref, o_ref, wbf_ref):
    @pl.when(pl.program_id(1) == 0)
    def _():
        wbf_ref[...] = w_ref[...].astype(BF)

    o_ref[...] = jnp.dot(x_ref[...], wbf_ref[...],
                         preferred_element_type=F32).astype(o_ref.dtype)


def matmul(x, w, layer, *, col0=0, ncols=None, tn=512, tm=1024, out_dtype=BF, name="matmul"):
    M, K = x.shape
    ncols = w.shape[2] - col0 if ncols is None else ncols
    tm = min(tm, M)
    assert col0 % tn == 0 and ncols % tn == 0 and M % tm == 0
    c0 = col0 // tn
    return pl.pallas_call(
        _mm_kernel,
        grid=(ncols // tn, M // tm),
        in_specs=[pl.BlockSpec((tm, K), lambda j, i: (i, 0)),
                  pl.BlockSpec((None, K, tn), lambda j, i: (layer, 0, j + c0))],
        out_specs=pl.BlockSpec((tm, tn), lambda j, i: (i, j)),
        out_shape=jax.ShapeDtypeStruct((M, ncols), out_dtype),
        scratch_shapes=[pltpu.VMEM((K, tn), BF)],
        compiler_params=_params("arbitrary", "arbitrary"),
        name=name,
    )(x, w)


def _gu_kernel(x_ref, wg_ref, wu_ref, o_ref, wgb_ref, wub_ref):
    @pl.when(pl.program_id(1) == 0)
    def _():
        wgb_ref[...] = wg_ref[...].astype(BF)
        wub_ref[...] = wu_ref[...].astype(BF)

    x = x_ref[...]
    g = jnp.dot(x, wgb_ref[...], preferred_element_type=F32)
    u = jnp.dot(x, wub_ref[...], preferred_element_type=F32)
    o_ref[...] = (g * (1.0 / (1.0 + jnp.exp(-g))) * u).astype(o_ref.dtype)


def ffn_gate_up(x, w, layer, tn=512, tm=1024):
    M, K = x.shape
    nj = D_FF // tn
    return pl.pallas_call(
        _gu_kernel,
        grid=(nj, M // tm),
        in_specs=[pl.BlockSpec((tm, K), lambda j, i: (i, 0)),
                  pl.BlockSpec((None, K, tn), lambda j, i: (layer, 0, j)),
                  pl.BlockSpec((None, K, tn), lambda j, i: (layer, 0, j + nj))],
        out_specs=pl.BlockSpec((tm, tn), lambda j, i: (i, j)),
        out_shape=jax.ShapeDtypeStruct((M, D_FF), BF),
        scratch_shapes=[pltpu.VMEM((K, tn), BF), pltpu.VMEM((K, tn), BF)],
        compiler_params=_params("arbitrary", "arbitrary"),
        name="ffn_gate_up",
    )(x, w, w)


def _proj_res_kernel(nk1, nk, has_x2, has_next, *refs):
    refs = list(refs)
    x1_ref = refs.pop(0)
    x2_ref = refs.pop(0) if has_x2 else None
    w_ref, xres_ref, g_ref = refs.pop(0), refs.pop(0), refs.pop(0)
    gn_ref = refs.pop(0) if has_next else None
    xo_ref = refs.pop(0)
    uo_ref = refs.pop(0) if has_next else None
    acc_ref = refs.pop(0)
    k = pl.program_id(1)

    @pl.when(k == 0)
    def _():
        acc_ref[...] = jnp.zeros_like(acc_ref)

    @pl.when(k < nk1)
    def _():
        acc_ref[...] += jnp.dot(x1_ref[...], w_ref[...], preferred_element_type=F32)

    if has_x2:
        @pl.when(k >= nk1)
        def _():
            acc_ref[...] += jnp.dot(x2_ref[...], w_ref[...], preferred_element_type=F32)

    @pl.when(k == nk - 1)
    def _():
        xn = xres_ref[...] + _rms(acc_ref[...], g_ref[...])
        xo_ref[...] = xn
        if has_next:
            uo_ref[...] = _rms(xn, gn_ref[...]).astype(uo_ref.dtype)


def proj_residual_norm(x1, x2, x2_col0, k2, w, layer, xres, g, gn, tm=512, tk=512):
    M, K1 = x1.shape
    N = w.shape[2]
    has_x2, has_next = x2 is not None, gn is not None
    nk1 = K1 // tk
    nk = nk1 + (k2 // tk if has_x2 else 0)
    c2 = x2_col0 // tk
    in_specs = [pl.BlockSpec((tm, tk), lambda i, k: (i, jnp.minimum(k, nk1 - 1)))]
    args = [x1]
    if has_x2:
        in_specs.append(pl.BlockSpec((tm, tk), lambda i, k: (i, jnp.maximum(k - nk1, 0) + c2)))
        args.append(x2)
    in_specs += [pl.BlockSpec((None, tk, N), lambda i, k: (layer, k, 0)),
                 pl.BlockSpec((tm, N), lambda i, k: (i, 0)),
                 pl.BlockSpec((1, N), lambda i, k: (0, 0))]
    args += [w, xres, g.reshape(1, N)]
    out_specs = [pl.BlockSpec((tm, N), lambda i, k: (i, 0))]
    out_shape = [jax.ShapeDtypeStruct((M, N), F32)]
    if has_next:
        in_specs.append(pl.BlockSpec((1, N), lambda i, k: (0, 0)))
        args.append(gn.reshape(1, N))
        out_specs.append(pl.BlockSpec((tm, N), lambda i, k: (i, 0)))
        out_shape.append(jax.ShapeDtypeStruct((M, N), BF))
    res = pl.pallas_call(
        functools.partial(_proj_res_kernel, nk1, nk, has_x2, has_next),
        grid=(M // tm, nk),
        in_specs=in_specs,
        out_specs=out_specs,
        out_shape=out_shape,
        scratch_shapes=[pltpu.VMEM((tm, N), F32)],
        compiler_params=_params("arbitrary", "arbitrary"),
        name="proj_residual_norm",
    )(*args)
    return (res[0], res[1]) if has_next else (res[0], None)


def _dil_kernel(d, max_dist, slopes, has_prev, *refs):
    if has_prev:
        q_ref, kp_ref, kc_ref, vp_ref, vc_ref, o_ref, l_ref = refs
    else:
        q_ref, kc_ref, vc_ref, o_ref, l_ref = refs
    blk = A_BLOCK
    i = pl.program_id(2)
    nk = 2 * blk if has_prev else blk
    qpos = lax.broadcasted_iota(jnp.int32, (blk, nk), 0)
    kidx = lax.broadcasted_iota(jnp.int32, (blk, nk), 1)
    if has_prev:
        dist = qpos + blk - kidx
        valid = (dist >= 0) & (dist <= max_dist) & ((kidx >= blk) | (i > 0))
    else:
        dist = qpos - kidx
        valid = (dist >= 0) & (dist <= max_dist)
    distf = (dist * d).astype(F32)
    for h in range(A_HEADS_PER_GROUP):
        c = slice(h * HEAD_DIM, (h + 1) * HEAD_DIM)
        q = q_ref[0, :, c]
        if has_prev:
            k = jnp.concatenate([kp_ref[0, :, c], kc_ref[0, :, c]], axis=0)
            v = jnp.concatenate([vp_ref[0, :, c], vc_ref[0, :, c]], axis=0)
        else:
            k, v = kc_ref[0, :, c], vc_ref[0, :, c]
        s = lax.dot_general(q, k, NT, preferred_element_type=F32) * SCALE - slopes[h] * distf
        p, lse = _masked_softmax(s, valid)
        o_ref[0, :, c] = jnp.dot(p.astype(BF), v, preferred_element_type=F32)
        l_ref[0, :, c] = jnp.broadcast_to(lse, (blk, HEAD_DIM))


def dilated_group(q, k, v, gi):
    B, S, C = q.shape
    w, d = DIL_GROUPS[gi]
    L = S // d
    nqb = L // A_BLOCK
    max_dist = w // d
    assert max_dist <= A_BLOCK
    has_prev = nqb > 1
    n_heads = len(DIL_GROUPS) * A_HEADS_PER_GROUP
    slopes = tuple(2.0 ** (-8.0 * (gi * A_HEADS_PER_GROUP + h + 1) / n_heads)
                   for h in range(A_HEADS_PER_GROUP))
    gw = A_HEADS_PER_GROUP * HEAD_DIM
    cpb = C // gw
    view = lambda t: t.reshape(B, L, d * C)
    cur = pl.BlockSpec((1, A_BLOCK, gw), lambda b, r, i: (b, i, r * cpb + gi))
    prev = pl.BlockSpec((1, A_BLOCK, gw), lambda b, r, i: (b, jnp.maximum(i - 1, 0), r * cpb + gi))
    if has_prev:
        in_specs, args = [cur, prev, cur, prev, cur], [view(q), view(k), view(k), view(v), view(v)]
    else:
        in_specs, args = [cur, cur, cur], [view(q), view(k), view(v)]
    out_spec = pl.BlockSpec((1, A_BLOCK, gw), lambda b, r, i: (b, i, r))
    o, lse = pl.pallas_call(
        functools.partial(_dil_kernel, d, max_dist, slopes, has_prev),
        grid=(B, d, nqb),
        in_specs=in_specs,
        out_specs=[out_spec, out_spec],
        out_shape=[jax.ShapeDtypeStruct((B, L, d * gw), F32)] * 2,
        compiler_params=_params("arbitrary", "arbitrary", "arbitrary"),
        name=f"dilated_d{d}",
    )(*args)
    return o.reshape(B, S, gw), lse.reshape(B, S, gw)


def _mix_kernel(o1, o2, o3, l1, l2, l3, out_ref):
    a, b, c = l1[...], l2[...], l3[...]
    m = jnp.maximum(jnp.maximum(a, b), c)
    ea, eb, ec = jnp.exp(a - m), jnp.exp(b - m), jnp.exp(c - m)
    num = ea * o1[...] + eb * o2[...] + ec * o3[...]
    out_ref[...] = (num * (1.0 / (ea + eb + ec))).astype(out_ref.dtype)


def mix_groups(os_, ls_, tq=256):
    B, S, C = os_[0].shape
    spec = pl.BlockSpec((1, tq, C), lambda b, i: (b, i, 0))
    return pl.pallas_call(
        _mix_kernel,
        grid=(B, S // tq),
        in_specs=[spec] * 6,
        out_specs=spec,
        out_shape=jax.ShapeDtypeStruct((B, S, C), BF),
        compiler_params=_params("arbitrary", "arbitrary"),
        name="dilated_mix",
    )(*os_, *ls_)


def _mem_attn_kernel(q_ref, kv_ref, o_ref):
    for h in range(MEM_HEADS):
        c = slice(h * HEAD_DIM, (h + 1) * HEAD_DIM)
        k = kv_ref[0, :, c]
        v = kv_ref[0, :, MEM_HEADS * HEAD_DIM + h * HEAD_DIM:MEM_HEADS * HEAD_DIM + (h + 1) * HEAD_DIM]
        s = lax.dot_general(q_ref[0, :, c], k, NT, preferred_element_type=F32) * SCALE
        m = jnp.max(s, axis=-1, keepdims=True)
        e = jnp.exp(s - m)
        p = e * (1.0 / jnp.sum(e, axis=-1, keepdims=True))
        o_ref[0, :, c] = jnp.dot(p.astype(BF), v, preferred_element_type=F32).astype(o_ref.dtype)


def memory_attention(q, q_col0, kv, tq=512):
    B, S, _ = q.shape
    w = MEM_HEADS * HEAD_DIM
    cb = q_col0 // w
    return pl.pallas_call(
        _mem_attn_kernel,
        grid=(B, S // tq),
        in_specs=[pl.BlockSpec((1, tq, w), lambda b, i: (b, i, cb)),
                  pl.BlockSpec((1, N_MEM, 2 * w), lambda b, i: (b, 0, 0))],
        out_specs=pl.BlockSpec((1, tq, w), lambda b, i: (b, i, 0)),
        out_shape=jax.ShapeDtypeStruct((B, S, w), BF),
        compiler_params=_params("arbitrary", "arbitrary"),
        name="memory_attention",
    )(q, kv)


def _gelu_tanh(x):
    return 0.5 * x * (1.0 + jnp.tanh(math.sqrt(2.0 / math.pi) * (x + 0.044715 * (x * x * x))))


def _compress_kernel(sec, n_sec, x_ref, w1_ref, w2_ref, pe_ref, o_ref, w1b_ref):
    @pl.when(pl.program_id(0) == 0)
    def _():
        w1b_ref[...] = w1_ref[...].astype(BF)

    w2 = w2_ref[...].astype(BF)
    nchunk = x_ref.shape[1]
    G = B_KV_GROUPS
    for g in range(G):
        top = jnp.zeros((nchunk, CMP_HIDDEN), F32)
        bot = jnp.zeros((nchunk, CMP_HIDDEN), F32)
        for l in range(CMP_STRIDE):
            c0 = ((l * n_sec + sec) * G + g) * HEAD_DIM
            xl = x_ref[0, :, c0:c0 + HEAD_DIM]
            lo, hi = l, CMP_STRIDE + l
            top += jnp.dot((xl + pe_ref[lo:lo + 1, :]).astype(BF),
                           w1b_ref[lo * HEAD_DIM:(lo + 1) * HEAD_DIM, :], preferred_element_type=F32)
            bot += jnp.dot((xl + pe_ref[hi:hi + 1, :]).astype(BF),
                           w1b_ref[hi * HEAD_DIM:(hi + 1) * HEAD_DIM, :], preferred_element_type=F32)
        hid = _gelu_tanh(top + pltpu.roll(bot, nchunk - 1, 0))
        o_ref[0, :, g * HEAD_DIM:(g + 1) * HEAD_DIM] = jnp.dot(
            hid.astype(BF), w2, preferred_element_type=F32).astype(o_ref.dtype)


def compress_blocks(kvc, sec, w1, w2, pe):
    B, S, C = kvc.shape
    assert CMP_LEN == 2 * CMP_STRIDE
    n_sec = C // (B_KV_GROUPS * HEAD_DIM)
    nchunk = S // CMP_STRIDE
    gw = B_KV_GROUPS * HEAD_DIM
    return pl.pallas_call(
        functools.partial(_compress_kernel, sec, n_sec),
        grid=(B,),
        in_specs=[pl.BlockSpec((1, nchunk, CMP_STRIDE * C), lambda b: (b, 0, 0)),
                  pl.BlockSpec(w1.shape, lambda b: (0, 0)),
                  pl.BlockSpec(w2.shape, lambda b: (0, 0)),
                  pl.BlockSpec(pe.shape, lambda b: (0, 0))],
        out_specs=pl.BlockSpec((1, nchunk, gw), lambda b: (b, 0, 0)),
        out_shape=jax.ShapeDtypeStruct((B, nchunk, gw), BF),
        scratch_shapes=[pltpu.VMEM(w1.shape, BF)],
        compiler_params=_params("arbitrary"),
        name=f"compress_{sec}",
    )(kvc.reshape(B, nchunk, CMP_STRIDE * C), w1, w2, pe)


def _nsa_kernel(tq, nwin, q_ref, gate_ref, kc_ref, vc_ref, ks_ref, vs_ref, kw_ref, vw_ref,
                sl_ref, o_ref):
    S = ks_ref.shape[1]
    ncmp = kc_ref.shape[1]
    nsel = S // SLC_LEN
    i = pl.program_id(2)
    t0 = i * tq
    t_col = t0 + lax.broadcasted_iota(jnp.int32, (tq, 1), 0)
    t_row = t0 + lax.broadcasted_iota(jnp.int32, (1, tq), 1)

    jj = lax.broadcasted_iota(jnp.int32, (nsel, ncmp), 0) * SLC_LEN
    cc = lax.broadcasted_iota(jnp.int32, (nsel, ncmp), 1) * CMP_STRIDE
    ov_t = jnp.maximum(jnp.minimum(cc + CMP_LEN, jj + SLC_LEN) - jnp.maximum(cc, jj), 0).astype(F32) / CMP_LEN

    qs = [q_ref[0, :, r * HEAD_DIM:(r + 1) * HEAD_DIM] for r in range(B_REP)]
    sl = [sl_ref[0, 0:1, r:r + 1] for r in range(B_REP)]

    kc, vc = kc_ref[0], vc_ref[0]
    cend = lax.broadcasted_iota(jnp.int32, (1, ncmp), 1) * CMP_STRIDE + (CMP_LEN - 1)
    dist_c = t_col - cend
    valid_c = dist_c >= 0
    dist_cf = dist_c.astype(F32)
    imp = jnp.zeros((nsel, tq), F32)
    o_cmp = []
    for r in range(B_REP):
        s = lax.dot_general(qs[r], kc, NT, preferred_element_type=F32) * SCALE - sl[r] * dist_cf
        p, _ = _masked_softmax(s, valid_c)
        o_cmp.append(jnp.dot(p.astype(BF), vc, preferred_element_type=F32))
        imp = imp + lax.dot_general(ov_t, p, NT, preferred_element_type=F32,
                                    precision=lax.Precision.HIGHEST)

    jb = lax.broadcasted_iota(jnp.int32, (nsel, tq), 0)
    cur = t_row // SLC_LEN
    forced = (jb == 0) | (jb == cur) | (jb == cur - 1)
    imp = jnp.where(forced, SEL_FORCE, jnp.where(jb > cur, -SEL_FORCE, imp))
    beaten_by = jnp.zeros((nsel, tq), jnp.int32)
    for j in range(nsel):
        row = imp[j:j + 1, :]
        beats = (row > imp) | ((row == imp) & (jb > j))
        beaten_by = beaten_by + beats.astype(jnp.int32)
    sel_t = (beaten_by < min(SLC_TOPK, nsel)).astype(F32)
    sel = jnp.concatenate([sel_t, jnp.zeros((LANES - nsel, tq), F32)], axis=0).T
    expand = (lax.broadcasted_iota(jnp.int32, (LANES, S), 1) // SLC_LEN
              == lax.broadcasted_iota(jnp.int32, (LANES, S), 0)).astype(BF)
    key_sel = jnp.dot(sel.astype(BF), expand, preferred_element_type=F32)

    ks, vs = ks_ref[0], vs_ref[0]
    dist_s = t_col - lax.broadcasted_iota(jnp.int32, (1, S), 1)
    valid_s = (dist_s >= 0) & (key_sel > 0.5)
    dist_sf = dist_s.astype(F32)
    o_slc = []
    for r in range(B_REP):
        s = lax.dot_general(qs[r], ks, NT, preferred_element_type=F32) * SCALE - sl[r] * dist_sf
        p, _ = _masked_softmax(s, valid_s)
        o_slc.append(jnp.dot(p.astype(BF), vs, preferred_element_type=F32))

    w0 = pl.multiple_of(jnp.maximum(t0 + tq - nwin, 0), tq)
    kw = kw_ref[0, pl.ds(w0, nwin), :]
    vw = vw_ref[0, pl.ds(w0, nwin), :]
    dist_w = t_col - (w0 + lax.broadcasted_iota(jnp.int32, (1, nwin), 1))
    valid_w = (dist_w >= 0) & (dist_w <= WIN_LEN - 1)
    dist_wf = dist_w.astype(F32)
    gate = 1.0 / (1.0 + jnp.exp(-gate_ref[0]))
    for r in range(B_REP):
        s = lax.dot_general(qs[r], kw, NT, preferred_element_type=F32) * SCALE - sl[r] * dist_wf
        p, _ = _masked_softmax(s, valid_w)
        o_win = jnp.dot(p.astype(BF), vw, preferred_element_type=F32)
        o = (gate[:, 3 * r:3 * r + 1] * o_cmp[r] + gate[:, 3 * r + 1:3 * r + 2] * o_slc[r]
             + gate[:, 3 * r + 2:3 * r + 3] * o_win)
        o_ref[0, :, r * HEAD_DIM:(r + 1) * HEAD_DIM] = o.astype(o_ref.dtype)


def nsa_attention(qp, gates, kc, vc, kv, tq=256):
    B, S, _ = qp.shape
    G = B_KV_GROUPS
    gq = B_REP * HEAD_DIM
    nwin = -(-(WIN_LEN - 1 + tq) // tq) * tq
    slopes = 2.0 ** (-8.0 * jnp.arange(1, B_HEADS + 1, dtype=F32) / B_HEADS)
    sl = jnp.zeros((G, 8, LANES), F32).at[:, :, :B_REP].set(
        jnp.broadcast_to(slopes.reshape(G, 1, B_REP), (G, 8, B_REP)))
    kvspec = lambda sec: pl.BlockSpec((1, S, HEAD_DIM), lambda b, g, i: (b, 0, sec * G + g))
    cspec = pl.BlockSpec((1, kc.shape[1], HEAD_DIM), lambda b, g, i: (b, 0, g))
    return pl.pallas_call(
        functools.partial(_nsa_kernel, tq, nwin),
        grid=(B, G, S // tq),
        in_specs=[pl.BlockSpec((1, tq, gq), lambda b, g, i: (b, i, g)),
                  pl.BlockSpec((1, tq, LANES), lambda b, g, i: (b, i, g)),
                  cspec, cspec, kvspec(0), kvspec(1), kvspec(2), kvspec(3),
                  pl.BlockSpec((1, 8, LANES), lambda b, g, i: (g, 0, 0))],
        out_specs=pl.BlockSpec((1, tq, gq), lambda b, g, i: (b, i, g)),
        out_shape=jax.ShapeDtypeStruct((B, S, B_HEADS * HEAD_DIM), BF),
        compiler_params=_params("arbitrary", "arbitrary", "arbitrary"),
        name="nsa_attention",
    )(qp, gates, kc, vc, kv, kv, kv, kv, sl)


def kernel(x, mem, norm_g, a_w_in, a_w_out, b_w_in, b_w_out, mem_w_kv, ffn_w_gu, ffn_w_down,
           kv_norm_g, kv_w, cmp_pe, cmp_wk1, cmp_wk2, cmp_wv1, cmp_wv2):
    B, S, D = x.shape
    M = B * S
    n_a = DEPTH // 2
    qa_cols = len(DIL_GROUPS) * A_HEADS_PER_GROUP * HEAD_DIM
    qb_cols = B_HEADS * HEAD_DIM
    mq_cols = MEM_HEADS * HEAD_DIM
    n_gate = 3 * B_HEADS
    G = B_KV_GROUPS

    b_w_qm = jnp.concatenate([b_w_in[:, :, :qb_cols], b_w_in[:, :, qb_cols + n_gate:]], axis=-1)
    b_w_gate = jnp.pad(b_w_in[:, :, qb_cols:qb_cols + n_gate].reshape(-1, D, G, n_gate // G),
                       ((0, 0), (0, 0), (0, 0), (0, LANES - n_gate // G))).reshape(-1, D, G * LANES)
    a_w_out_b = a_w_out.astype(BF)
    b_w_out_b = b_w_out.astype(BF)
    w_down_b = ffn_w_down.astype(BF)
    kv_w3 = kv_w[None]

    x2 = x.reshape(M, D)
    mem2 = mem.reshape(B * N_MEM, D)
    u = rmsnorm_rows(x2, norm_g[0, 0])
    shared = None
    for l in range(DEPTH):
        g = norm_g[l]
        mem_n = rmsnorm_rows(mem2, g[4])
        mkv = matmul(mem_n, mem_w_kv, l, name="mem_kv").reshape(B, N_MEM, 2 * mq_cols)
        if l < n_a:
            q = matmul(u, a_w_in, l, col0=0, ncols=qa_cols, name="a_q").reshape(B, S, qa_cols)
            k = matmul(u, a_w_in, l, col0=qa_cols, ncols=qa_cols, name="a_k").reshape(B, S, qa_cols)
            v = matmul(u, a_w_in, l, col0=2 * qa_cols, ncols=qa_cols, name="a_v").reshape(B, S, qa_cols)
            mq = matmul(u, a_w_in, l, col0=3 * qa_cols, ncols=mq_cols, name="a_mq").reshape(B, S, mq_cols)
            outs = [dilated_group(q, k, v, gi) for gi in range(len(DIL_GROUPS))]
            o_main = mix_groups([o for o, _ in outs], [ls for _, ls in outs])
            o_mem = memory_attention(mq, 0, mkv)
            w_out = a_w_out_b
            li = l
        else:
            li = l - n_a
            if shared is None:
                kvn = rmsnorm_rows(x2, kv_norm_g)
                kvc = matmul(kvn, kv_w3, 0, col0=0, ncols=2 * G * HEAD_DIM, out_dtype=F32,
                             name="kv_cmp").reshape(B, S, 2 * G * HEAD_DIM)
                kvr = matmul(kvn, kv_w3, 0, col0=2 * G * HEAD_DIM, ncols=4 * G * HEAD_DIM,
                             name="kv_rest").reshape(B, S, 4 * G * HEAD_DIM)
                kc = compress_blocks(kvc, 0, cmp_wk1, cmp_wk2, cmp_pe[0])
                vc = compress_blocks(kvc, 1, cmp_wv1, cmp_wv2, cmp_pe[1])
                shared = (kc, vc, kvr)
            qm = matmul(u, b_w_qm, li, name="b_qm").reshape(B, S, qb_cols + mq_cols)
            gates = matmul(u, b_w_gate, li, out_dtype=F32, name="b_gate").reshape(B, S, G * LANES)
            o_main = nsa_attention(qm, gates, *shared)
            o_mem = memory_attention(qm, qb_cols, mkv)
            w_out = b_w_out_b
        x2, u = proj_residual_norm(o_main.reshape(M, -1), o_mem.reshape(M, mq_cols), 0, mq_cols,
                                   w_out, li, x2, g[1], g[2])
        h = ffn_gate_up(u, ffn_w_gu, l)
        gn = norm_g[l + 1, 0] if l + 1 < DEPTH else None
        x2, u = proj_residual_norm(h, None, 0, 0, w_down_b, l, x2, g[3], gn)
    return x2.reshape(B, S, D)
```

```python
import functools
import math

import jax
import jax.numpy as jnp
from jax import lax
from jax.experimental import pallas as pl
from jax.experimental.pallas import tpu as pltpu

BF = jnp.bfloat16
F32 = jnp.float32

D_MODEL = 2048
DEPTH = 4
HEAD_DIM = 128
N_MEM = 256
MEM_HEADS = 4
DIL_GROUPS = ((128, 1), (512, 4), (2048, 16))
A_HEADS_PER_GROUP = 8
A_BLOCK = 128
B_HEADS = 12
B_KV_GROUPS = 4
B_REP = B_HEADS // B_KV_GROUPS
CMP_LEN = 32
CMP_STRIDE = 16
CMP_HIDDEN = 512
SLC_LEN = 64
SLC_TOPK = 16
SEL_FORCE = 1e4
WIN_LEN = 512
D_FF = 5632
EPS = 1e-6

LANES = 128
VMEM_LIMIT = 56 * 1024 * 1024
NEG = -1e30
SCALE = 1.0 / math.sqrt(HEAD_DIM)
LOG2E = math.log2(math.e)
LN2 = math.log(2.0)
SCALE2 = SCALE * LOG2E
NT = (((1,), (1,)), ((), ()))


def _params(*sem):
    return pltpu.CompilerParams(dimension_semantics=sem, vmem_limit_bytes=VMEM_LIMIT)


def _masked_softmax(s, valid):
    s = jnp.where(valid, s, NEG)
    m = jnp.max(s, axis=-1, keepdims=True)
    e = jnp.where(valid, jnp.exp(s - m), 0.0)
    den = jnp.maximum(jnp.sum(e, axis=-1, keepdims=True), 1e-30)
    return e * (1.0 / den)


def _rms(x, g):
    return x * lax.rsqrt(jnp.mean(x * x, axis=-1, keepdims=True) + EPS) * g


def _iota(shape, dim):
    return lax.broadcasted_iota(jnp.int32, shape, dim)


def _rmsnorm_kernel(x_ref, g_ref, o_ref):
    o_ref[...] = _rms(x_ref[...], g_ref[...]).astype(o_ref.dtype)


def rmsnorm_rows(x, g, tm=512):
    M, D = x.shape
    tm = min(tm, M)
    return pl.pallas_call(
        _rmsnorm_kernel,
        grid=(M // tm,),
        in_specs=[pl.BlockSpec((tm, D), lambda i: (i, 0)),
                  pl.BlockSpec((1, D), lambda i: (0, 0))],
        out_specs=pl.BlockSpec((tm, D), lambda i: (i, 0)),
        out_shape=jax.ShapeDtypeStruct((M, D), BF),
        compiler_params=_params("arbitrary"),
        name="rmsnorm",
    )(x, g.reshape(1, D))


def _mm_kernel(x_ref, w_ref, o_ref, wbf_ref):
    @pl.when(pl.program_id(1) == 0)
    def _():
        wbf_ref[...] = w_ref[...].astype(BF)

    o_ref[...] = jnp.dot(x_ref[...], wbf_ref[...],
                         preferred_element_type=F32).astype(o_ref.dtype)


def matmul(x, w, layer, *, col0=0, ncols=None, tn=512, tm=1024, out_dtype=BF, name="matmul"):
    M, K = x.shape
    ncols = w.shape[2] - col0 if ncols is None else ncols
    tm = min(tm, M)
    assert col0 % tn == 0 and ncols % tn == 0 and M % tm == 0
    c0 = col0 // tn
    return pl.pallas_call(
        _mm_kernel,
        grid=(ncols // tn, M // tm),
        in_specs=[pl.BlockSpec((tm, K), lambda j, i: (i, 0)),
                  pl.BlockSpec((None, K, tn), lambda j, i: (layer, 0, j + c0))],
        out_specs=pl.BlockSpec((tm, tn), lambda j, i: (i, j)),
        out_shape=jax.ShapeDtypeStruct((M, ncols), out_dtype),
        scratch_shapes=[pltpu.VMEM((K, tn), BF)],
        compiler_params=_params("arbitrary", "arbitrary"),
        name=name,
    )(x, w)


def _gu_kernel(x_ref, wg_ref, wu_ref, o_ref, wgb_ref, wub_ref):
    @pl.when(pl.program_id(1) == 0)
    def _():
        wgb_ref[...] = wg_ref[...].astype(BF)
        wub_ref[...] = wu_ref[...].astype(BF)

    x = x_ref[...]
    g = jnp.dot(x, wgb_ref[...], preferred_element_type=F32)
    u = jnp.dot(x, wub_ref[...], preferred_element_type=F32)
    o_ref[...] = (g * (1.0 / (1.0 + jnp.exp(-g))) * u).astype(o_ref.dtype)


def ffn_gate_up(x, w, layer, tn=512, tm=1024):
    M, K = x.shape
    nj = D_FF // tn
    return pl.pallas_call(
        _gu_kernel,
        grid=(nj, M // tm),
        in_specs=[pl.BlockSpec((tm, K), lambda j, i: (i, 0)),
                  pl.BlockSpec((None, K, tn), lambda j, i: (layer, 0, j)),
                  pl.BlockSpec((None, K, tn), lambda j, i: (layer, 0, j + nj))],
        out_specs=pl.BlockSpec((tm, tn), lambda j, i: (i, j)),
        out_shape=jax.ShapeDtypeStruct((M, D_FF), BF),
        scratch_shapes=[pltpu.VMEM((K, tn), BF), pltpu.VMEM((K, tn), BF)],
        compiler_params=_params("arbitrary", "arbitrary"),
        name="ffn_gate_up",
    )(x, w, w)


def _proj_res_kernel(nk1, nk, has_x2, has_next, *refs):
    refs = list(refs)
    x1_ref = refs.pop(0)
    x2_ref = refs.pop(0) if has_x2 else None
    w_ref, xres_ref, g_ref = refs.pop(0), refs.pop(0), refs.pop(0)
    gn_ref = refs.pop(0) if has_next else None
    xo_ref = refs.pop(0)
    uo_ref = refs.pop(0) if has_next else None
    acc_ref = refs.pop(0)
    k = pl.program_id(1)

    @pl.when(k == 0)
    def _():
        acc_ref[...] = jnp.zeros_like(acc_ref)

    @pl.when(k < nk1)
    def _():
        acc_ref[...] += jnp.dot(x1_ref[...], w_ref[...], preferred_element_type=F32)

    if has_x2:
        @pl.when(k >= nk1)
        def _():
            acc_ref[...] += jnp.dot(x2_ref[...], w_ref[...], preferred_element_type=F32)

    @pl.when(k == nk - 1)
    def _():
        xn = xres_ref[...] + _rms(acc_ref[...], g_ref[...])
        xo_ref[...] = xn
        if has_next:
            uo_ref[...] = _rms(xn, gn_ref[...]).astype(uo_ref.dtype)


def proj_residual_norm(x1, x2, x2_col0, k2, w, layer, xres, g, gn, tm=512, tk=512):
    M, K1 = x1.shape
    N = w.shape[2]
    has_x2, has_next = x2 is not None, gn is not None
    nk1 = K1 // tk
    nk = nk1 + (k2 // tk if has_x2 else 0)
    c2 = x2_col0 // tk
    in_specs = [pl.BlockSpec((tm, tk), lambda i, k: (i, jnp.minimum(k, nk1 - 1)))]
    args = [x1]
    if has_x2:
        in_specs.append(pl.BlockSpec((tm, tk), lambda i, k: (i, jnp.maximum(k - nk1, 0) + c2)))
        args.append(x2)
    in_specs += [pl.BlockSpec((None, tk, N), lambda i, k: (layer, k, 0)),
                 pl.BlockSpec((tm, N), lambda i, k: (i, 0)),
                 pl.BlockSpec((1, N), lambda i, k: (0, 0))]
    args += [w, xres, g.reshape(1, N)]
    out_specs = [pl.BlockSpec((tm, N), lambda i, k: (i, 0))]
    out_shape = [jax.ShapeDtypeStruct((M, N), F32)]
    if has_next:
        in_specs.append(pl.BlockSpec((1, N), lambda i, k: (0, 0)))
        args.append(gn.reshape(1, N))
        out_specs.append(pl.BlockSpec((tm, N), lambda i, k: (i, 0)))
        out_shape.append(jax.ShapeDtypeStruct((M, N), BF))
    res = pl.pallas_call(
        functools.partial(_proj_res_kernel, nk1, nk, has_x2, has_next),
        grid=(M // tm, nk),
        in_specs=in_specs,
        out_specs=out_specs,
        out_shape=out_shape,
        scratch_shapes=[pltpu.VMEM((tm, N), F32)],
        compiler_params=_params("arbitrary", "arbitrary"),
        name="proj_residual_norm",
    )(*args)
    return (res[0], res[1]) if has_next else (res[0], None)


def _dil_kernel(*refs):
    ng = len(DIL_GROUPS)
    q_refs, k_refs, v_refs = refs[:ng], refs[ng:2 * ng], refs[2 * ng:3 * ng]
    o_ref, o_scr, l_scr = refs[3 * ng:]
    S = o_ref.shape[1]
    blk = A_BLOCK
    n_heads = ng * A_HEADS_PER_GROUP
    j = pl.program_id(1)
    for gi, (w, d) in enumerate(DIL_GROUPS):
        q_ref, k_ref, v_ref = q_refs[gi], k_refs[gi], v_refs[gi]
        nqb = S // d // blk
        max_dist = w // d
        has_prev = nqb > 1
        nk = 2 * blk if has_prev else blk
        koff = blk if has_prev else 0
        head = (j + (gi * A_HEADS_PER_GROUP + 1)).astype(F32)
        slope = jnp.exp2(jnp.zeros((1, 1), F32) + head * (-8.0 / n_heads))
        kidx = _iota((blk, nk), 1)
        dist = _iota((blk, nk), 0) + koff - kidx
        band = (dist >= 0) & (dist <= max_dist)
        bias = jnp.where(band, (kidx - koff).astype(F32) * (slope * (d * LOG2E)), NEG)
        bias_first = jnp.where(kidx >= koff, bias, NEG)
        row_term = _iota((blk, 1), 0).astype(F32) * (slope * (-float(d)))

        def rows_at(start, d=d):
            return pl.ds(start, blk, stride=d) if d > 1 else pl.ds(start, blk)

        def body(n, carry, gi=gi, d=d, has_prev=has_prev, q_ref=q_ref, k_ref=k_ref, v_ref=v_ref,
                 bias=bias, bias_first=bias_first, row_term=row_term, rows_at=rows_at):
            r = n % d
            i = n // d
            start = r + i * (d * blk)
            rows = rows_at(start)
            q = q_ref[0, rows, :].astype(BF)
            if has_prev:
                prow = rows_at(jnp.where(i > 0, start - d * blk, start))
                k = jnp.concatenate([k_ref[0, prow, :], k_ref[0, rows, :]], axis=0).astype(BF)
                v = jnp.concatenate([v_ref[0, prow, :], v_ref[0, rows, :]], axis=0).astype(BF)
                b = jnp.where(i > 0, bias, bias_first)
            else:
                k, v, b = k_ref[0, rows, :].astype(BF), v_ref[0, rows, :].astype(BF), bias
            s = lax.dot_general(q, k, NT, preferred_element_type=F32) * SCALE2 + b
            m = jnp.max(s, axis=-1, keepdims=True)
            p = jnp.exp2(s - m)
            l = jnp.sum(p, axis=-1, keepdims=True)
            o = jnp.dot(p.astype(BF), v, preferred_element_type=F32) * (1.0 / l)
            lse = (m + jnp.log2(l)) * LN2 + row_term
            o_scr[gi, rows, :] = o
            l_scr[gi, rows, :] = jnp.broadcast_to(lse, (blk, HEAD_DIM))
            return carry

        lax.fori_loop(0, d * nqb, body, 0, unroll=8)

    tr = 256
    for c in range(S // tr):
        rows = slice(c * tr, (c + 1) * tr)
        ls = [l_scr[gi, rows, :] for gi in range(ng)]
        m = functools.reduce(jnp.maximum, ls)
        es = [jnp.exp(x - m) for x in ls]
        num = sum(e * o_scr[gi, rows, :] for gi, e in enumerate(es))
        o_ref[0, rows, :] = (num * (1.0 / sum(es))).astype(o_ref.dtype)


def dilated_attention(q, k, v):
    B, S, _ = q.shape
    ng, Hg = len(DIL_GROUPS), A_HEADS_PER_GROUP
    for w, d in DIL_GROUPS:
        assert w // d <= A_BLOCK and S % (d * A_BLOCK) == 0
    specs = [pl.BlockSpec((1, S, HEAD_DIM), lambda b, j, gi=gi: (b, 0, gi * Hg + j)) for gi in range(ng)]
    return pl.pallas_call(
        _dil_kernel,
        grid=(B, Hg),
        in_specs=specs * 3,
        out_specs=pl.BlockSpec((1, S, HEAD_DIM), lambda b, j: (b, 0, j)),
        out_shape=jax.ShapeDtypeStruct((B, S, Hg * HEAD_DIM), BF),
        scratch_shapes=[pltpu.VMEM((ng, S, HEAD_DIM), F32), pltpu.VMEM((ng, S, HEAD_DIM), F32)],
        compiler_params=_params("arbitrary", "arbitrary"),
        name="dilated_attention",
    )(*([q] * ng + [k] * ng + [v] * ng))


def _mem_attn_kernel(q_ref, kv_ref, o_ref):
    for h in range(MEM_HEADS):
        c = slice(h * HEAD_DIM, (h + 1) * HEAD_DIM)
        k = kv_ref[0, :, c]
        v = kv_ref[0, :, MEM_HEADS * HEAD_DIM + h * HEAD_DIM:MEM_HEADS * HEAD_DIM + (h + 1) * HEAD_DIM]
        s = lax.dot_general(q_ref[0, :, c], k, NT, preferred_element_type=F32) * SCALE
        m = jnp.max(s, axis=-1, keepdims=True)
        e = jnp.exp(s - m)
        p = e * (1.0 / jnp.sum(e, axis=-1, keepdims=True))
        o_ref[0, :, c] = jnp.dot(p.astype(BF), v, preferred_element_type=F32).astype(o_ref.dtype)


def memory_attention(q, q_col0, kv, tq=512):
    B, S, _ = q.shape
    w = MEM_HEADS * HEAD_DIM
    cb = q_col0 // w
    return pl.pallas_call(
        _mem_attn_kernel,
        grid=(B, S // tq),
        in_specs=[pl.BlockSpec((1, tq, w), lambda b, i: (b, i, cb)),
                  pl.BlockSpec((1, N_MEM, 2 * w), lambda b, i: (b, 0, 0))],
        out_specs=pl.BlockSpec((1, tq, w), lambda b, i: (b, i, 0)),
        out_shape=jax.ShapeDtypeStruct((B, S, w), BF),
        compiler_params=_params("arbitrary", "arbitrary"),
        name="memory_attention",
    )(q, kv)


def _gelu_tanh(x):
    return 0.5 * x * (1.0 + jnp.tanh(math.sqrt(2.0 / math.pi) * (x + 0.044715 * (x * x * x))))


def _compress_kernel(x_ref, w1_ref, w2_ref, pe_ref, o_ref, w1b_ref):
    @pl.when((pl.program_id(0) == 0) & (pl.program_id(1) == 0))
    def _():
        w1b_ref[...] = w1_ref[...].astype(BF)

    nchunk = o_ref.shape[1]
    top = jnp.zeros((nchunk, CMP_HIDDEN), F32)
    bot = jnp.zeros((nchunk, CMP_HIDDEN), F32)
    for l in range(CMP_STRIDE):
        xl = x_ref[0, pl.ds(l, nchunk, stride=CMP_STRIDE), :]
        lo, hi = l, CMP_STRIDE + l
        top += jnp.dot((xl + pe_ref[lo:lo + 1, :]).astype(BF),
                       w1b_ref[lo * HEAD_DIM:(lo + 1) * HEAD_DIM, :], preferred_element_type=F32)
        bot += jnp.dot((xl + pe_ref[hi:hi + 1, :]).astype(BF),
                       w1b_ref[hi * HEAD_DIM:(hi + 1) * HEAD_DIM, :], preferred_element_type=F32)
    hid = _gelu_tanh(top + pltpu.roll(bot, nchunk - 1, 0))
    o_ref[0] = jnp.dot(hid.astype(BF), w2_ref[...].astype(BF),
                       preferred_element_type=F32).astype(o_ref.dtype)


def compress_blocks(kvc, sec, w1, w2, pe):
    B, S, C = kvc.shape
    assert CMP_LEN == 2 * CMP_STRIDE
    G = B_KV_GROUPS
    nchunk = S // CMP_STRIDE
    return pl.pallas_call(
        _compress_kernel,
        grid=(B, G),
        in_specs=[pl.BlockSpec((1, S, HEAD_DIM), lambda b, g: (b, 0, sec * G + g)),
                  pl.BlockSpec(w1.shape, lambda b, g: (0, 0)),
                  pl.BlockSpec(w2.shape, lambda b, g: (0, 0)),
                  pl.BlockSpec(pe.shape, lambda b, g: (0, 0))],
        out_specs=pl.BlockSpec((1, nchunk, HEAD_DIM), lambda b, g: (b, 0, g)),
        out_shape=jax.ShapeDtypeStruct((B, nchunk, G * HEAD_DIM), BF),
        scratch_shapes=[pltpu.VMEM(w1.shape, BF)],
        compiler_params=_params("arbitrary", "arbitrary"),
        name=f"compress_{sec}",
    )(kvc, w1, w2, pe)


ONES_ROWS = 16


def _t_bf(x):
    return x.astype(F32).T.astype(BF)


def _nsa_kernel(tq, q_ref, gate_ref, kc_ref, vc_ref, ks_ref, vs_ref, kw_ref, vw_ref,
                sl_ref, o_ref, vst_ref, vwt_ref, vct_ref, neg_ref, raw_ref):
    S = ks_ref.shape[1]
    ncmp = kc_ref.shape[1]
    nsel = S // SLC_LEN
    i = pl.program_id(2)
    t_row = i * tq + _iota((1, tq), 1)

    @pl.when(i == 0)
    def _():
        for src_ref, dst_ref in ((vs_ref, vst_ref), (vw_ref, vwt_ref)):
            n, _, w = dst_ref.shape
            ones = jnp.ones((ONES_ROWS, w), BF)
            for c in range(n):
                dst_ref[c] = jnp.concatenate([_t_bf(src_ref[0, c * w:(c + 1) * w, :]), ones], axis=0)
        vct_ref[...] = _t_bf(vc_ref[0])

    q_t = [_t_bf(q_ref[0, :, r * HEAD_DIM:(r + 1) * HEAD_DIM]) for r in range(B_REP)]
    sl = [sl_ref[0, 0:1, r:r + 1] for r in range(B_REP)]
    sl2 = [s * LOG2E for s in sl]

    jj = _iota((nsel, ncmp), 0) * SLC_LEN
    cc = _iota((nsel, ncmp), 1) * CMP_STRIDE
    ov_t = jnp.maximum(jnp.minimum(cc + CMP_LEN, jj + SLC_LEN) - jnp.maximum(cc, jj), 0).astype(F32) / CMP_LEN

    kc = kc_ref[0]
    dist_c = t_row - (_iota((ncmp, 1), 0) * CMP_STRIDE + (CMP_LEN - 1))
    valid_c = dist_c >= 0
    dist_cf = dist_c.astype(F32)
    imp = jnp.zeros((nsel, tq), F32)
    o_cmp = []
    for r in range(B_REP):
        s = jnp.dot(kc, q_t[r], preferred_element_type=F32) * SCALE - sl[r] * dist_cf
        s = jnp.where(valid_c, s, NEG)
        e = jnp.where(valid_c, jnp.exp(s - jnp.max(s, axis=0, keepdims=True)), 0.0)
        p = e * (1.0 / jnp.maximum(jnp.sum(e, axis=0, keepdims=True), 1e-30))
        o_cmp.append(jnp.dot(vct_ref[...], p.astype(BF), preferred_element_type=F32))
        imp = imp + jnp.dot(ov_t, p, preferred_element_type=F32, precision=lax.Precision.HIGHEST)

    jb = _iota((nsel, tq), 0)
    cur = t_row // SLC_LEN
    forced = (jb == 0) | (jb == cur) | (jb == cur - 1)
    imp = jnp.where(forced, SEL_FORCE, jnp.where(jb > cur, -SEL_FORCE, imp))
    beaten_by = jnp.zeros((nsel, tq), jnp.int32)
    for j in range(nsel):
        row = imp[j:j + 1, :]
        beats = (row > imp) | ((row == imp) & (jb > j))
        beaten_by = beaten_by + beats.astype(jnp.int32)
    neg_ref[...] = jnp.where(beaten_by >= min(SLC_TOPK, nsel), NEG, 0.0)

    n_back = -(-(WIN_LEN - 1) // tq)
    nwin = (n_back + 1) * tq
    kcs = vst_ref.shape[2]
    bps = kcs // SLC_LEN
    t0 = i * tq

    def slab_terms(nk):
        key_i = _iota((nk, tq), 0)
        key_f = key_i.astype(F32)
        return key_i, t0 + _iota((nk, tq), 1), [key_f * s for s in sl2]

    def softmax_step(raw, vt_c, bias, c_off, st):
        m, acc = st
        s = raw * SCALE2 + bias
        m_new = jnp.maximum(m, jnp.max(s, axis=0, keepdims=True) + c_off)
        p = jnp.exp2(s - (m_new - c_off))
        acc = jnp.exp2(m - m_new) * acc + jnp.dot(vt_c, p.astype(BF), preferred_element_type=F32)
        return m_new, acc

    def finish(st):
        _, acc = st
        return acc[:HEAD_DIM] * (1.0 / acc[HEAD_DIM:HEAD_DIM + 1])

    init = (jnp.full((1, tq), NEG, F32), jnp.zeros((HEAD_DIM + ONES_ROWS, tq), F32))

    c0 = jnp.maximum(i - n_back, 0)
    wbase = pl.multiple_of(c0 * tq, tq)
    key_w, qry_w, alibi_w = slab_terms(nwin)
    dist = qry_w - (wbase + key_w)
    wmask = jnp.where((dist >= 0) & (dist <= WIN_LEN - 1), 0.0, NEG)
    kw = kw_ref[0, pl.ds(wbase, nwin), :]
    vwt = jnp.concatenate([vwt_ref[c0 + j] for j in range(n_back + 1)], axis=1)
    stw = tuple(softmax_step(jnp.dot(kw, q_t[r], preferred_element_type=F32), vwt, wmask + alibi_w[r],
                             jnp.zeros((1, 1), F32), init) for r in range(B_REP))

    key_s, qry_s, alibi_s = slab_terms(kcs)
    n_slab = (t0 + tq + kcs - 1) // kcs

    def raw_scores(n, slot):
        k_c = ks_ref[0, pl.ds(pl.multiple_of(n * kcs, kcs), kcs), :]
        for r in range(B_REP):
            raw_ref[slot, r] = jnp.dot(k_c, q_t[r], preferred_element_type=F32)

    raw_scores(0, 0)

    def sel_body(n, sts):
        slot = n % 2
        raw_scores(jnp.minimum(n + 1, n_slab - 1), 1 - slot)
        base = n * kcs
        rows = [jnp.broadcast_to(neg_ref[pl.ds(n * bps + b, 1), :], (SLC_LEN, tq)) for b in range(bps)]
        mask = jnp.concatenate(rows, axis=0) + jnp.where(base + key_s <= qry_s, 0.0, NEG)
        vt_c = vst_ref[n]
        return tuple(softmax_step(raw_ref[slot, r], vt_c, mask + alibi_s[r],
                                  (base - t0).astype(F32) * sl2[r], sts[r]) for r in range(B_REP))

    sts = lax.fori_loop(0, n_slab, sel_body, (init,) * B_REP)

    gate_t = (1.0 / (1.0 + jnp.exp(-gate_ref[0]))).T
    for r in range(B_REP):
        o_t = (gate_t[3 * r:3 * r + 1] * o_cmp[r] + gate_t[3 * r + 1:3 * r + 2] * finish(sts[r])
               + gate_t[3 * r + 2:3 * r + 3] * finish(stw[r]))
        o_ref[0, :, r * HEAD_DIM:(r + 1) * HEAD_DIM] = o_t.T.astype(o_ref.dtype)


def nsa_attention(qp, gates, kc, vc, kv, tq=256):
    B, S, _ = qp.shape
    G = B_KV_GROUPS
    gq = B_REP * HEAD_DIM
    ncmp = kc.shape[1]
    kcs = 2 * tq
    nsel = S // SLC_LEN
    assert kcs % SLC_LEN == 0 and S % kcs == 0 and S >= (-(-(WIN_LEN - 1) // tq) + 1) * tq
    slopes = 2.0 ** (-8.0 * jnp.arange(1, B_HEADS + 1, dtype=F32) / B_HEADS)
    sl = jnp.zeros((G, 8, LANES), F32).at[:, :, :B_REP].set(
        jnp.broadcast_to(slopes.reshape(G, 1, B_REP), (G, 8, B_REP)))
    kvspec = lambda sec: pl.BlockSpec((1, S, HEAD_DIM), lambda b, g, i: (b, 0, sec * G + g))
    cspec = pl.BlockSpec((1, ncmp, HEAD_DIM), lambda b, g, i: (b, 0, g))
    vt_rows = HEAD_DIM + ONES_ROWS
    return pl.pallas_call(
        functools.partial(_nsa_kernel, tq),
        grid=(B, G, S // tq),
        in_specs=[pl.BlockSpec((1, tq, gq), lambda b, g, i: (b, i, g)),
                  pl.BlockSpec((1, tq, LANES), lambda b, g, i: (b, i, g)),
                  cspec, cspec, kvspec(0), kvspec(1), kvspec(2), kvspec(3),
                  pl.BlockSpec((1, 8, LANES), lambda b, g, i: (g, 0, 0))],
        out_specs=pl.BlockSpec((1, tq, gq), lambda b, g, i: (b, i, g)),
        out_shape=jax.ShapeDtypeStruct((B, S, B_HEADS * HEAD_DIM), BF),
        scratch_shapes=[pltpu.VMEM((S // kcs, vt_rows, kcs), BF), pltpu.VMEM((S // tq, vt_rows, tq), BF),
                        pltpu.VMEM((HEAD_DIM, ncmp), BF), pltpu.VMEM((nsel, tq), F32),
                        pltpu.VMEM((2, B_REP, kcs, tq), F32)],
        compiler_params=_params("arbitrary", "arbitrary", "arbitrary"),
        name="nsa_attention",
    )(qp, gates, kc, vc, kv, kv, kv, kv, sl)


def kernel(x, mem, norm_g, a_w_in, a_w_out, b_w_in, b_w_out, mem_w_kv, ffn_w_gu, ffn_w_down,
           kv_norm_g, kv_w, cmp_pe, cmp_wk1, cmp_wk2, cmp_wv1, cmp_wv2):
    B, S, D = x.shape
    M = B * S
    n_a = DEPTH // 2
    qa_cols = len(DIL_GROUPS) * A_HEADS_PER_GROUP * HEAD_DIM
    qb_cols = B_HEADS * HEAD_DIM
    mq_cols = MEM_HEADS * HEAD_DIM
    n_gate = 3 * B_HEADS
    G = B_KV_GROUPS

    b_w_qm = jnp.concatenate([b_w_in[:, :, :qb_cols], b_w_in[:, :, qb_cols + n_gate:]], axis=-1)
    b_w_gate = jnp.pad(b_w_in[:, :, qb_cols:qb_cols + n_gate].reshape(-1, D, G, n_gate // G),
                       ((0, 0), (0, 0), (0, 0), (0, LANES - n_gate // G))).reshape(-1, D, G * LANES)
    a_w_out_b = a_w_out.astype(BF)
    b_w_out_b = b_w_out.astype(BF)
    w_down_b = ffn_w_down.astype(BF)
    kv_w3 = kv_w[None]

    x2 = x.reshape(M, D)
    mem2 = mem.reshape(B * N_MEM, D)
    u = rmsnorm_rows(x2, norm_g[0, 0])
    shared = None
    for l in range(DEPTH):
        g = norm_g[l]
        mem_n = rmsnorm_rows(mem2, g[4])
        mkv = matmul(mem_n, mem_w_kv, l, name="mem_kv").reshape(B, N_MEM, 2 * mq_cols)
        if l < n_a:
            q = matmul(u, a_w_in, l, col0=0, ncols=qa_cols, out_dtype=F32, name="a_q").reshape(B, S, qa_cols)
            k = matmul(u, a_w_in, l, col0=qa_cols, ncols=qa_cols, out_dtype=F32, name="a_k").reshape(B, S, qa_cols)
            v = matmul(u, a_w_in, l, col0=2 * qa_cols, ncols=qa_cols, out_dtype=F32,
                       name="a_v").reshape(B, S, qa_cols)
            mq = matmul(u, a_w_in, l, col0=3 * qa_cols, ncols=mq_cols, name="a_mq").reshape(B, S, mq_cols)
            o_main = dilated_attention(q, k, v)
            o_mem = memory_attention(mq, 0, mkv)
            w_out = a_w_out_b
            li = l
        else:
            li = l - n_a
            if shared is None:
                kvn = rmsnorm_rows(x2, kv_norm_g)
                kvc = matmul(kvn, kv_w3, 0, col0=0, ncols=2 * G * HEAD_DIM, out_dtype=F32,
                             name="kv_cmp").reshape(B, S, 2 * G * HEAD_DIM)
                kvr = matmul(kvn, kv_w3, 0, col0=2 * G * HEAD_DIM, ncols=4 * G * HEAD_DIM,
                             name="kv_rest").reshape(B, S, 4 * G * HEAD_DIM)
                kc = compress_blocks(kvc, 0, cmp_wk1, cmp_wk2, cmp_pe[0])
                vc = compress_blocks(kvc, 1, cmp_wv1, cmp_wv2, cmp_pe[1])
                shared = (kc, vc, kvr)
            qm = matmul(u, b_w_qm, li, name="b_qm").reshape(B, S, qb_cols + mq_cols)
            gates = matmul(u, b_w_gate, li, out_dtype=F32, name="b_gate").reshape(B, S, G * LANES)
            o_main = nsa_attention(qm, gates, *shared)
            o_mem = memory_attention(qm, qb_cols, mkv)
            w_out = b_w_out_b
        x2, u = proj_residual_norm(o_main.reshape(M, -1), o_mem.reshape(M, mq_cols), 0, mq_cols,
                                   w_out, li, x2, g[1], g[2])
        h = ffn_gate_up(u, ffn_w_gu, l)
        gn = norm_g[l + 1, 0] if l + 1 < DEPTH else None
        x2, u = proj_residual_norm(h, None, 0, 0, w_down_b, l, x2, g[3], gn)
    return x2.reshape(B, S, D)
```

```python
import functools
import math

import jax
import jax.numpy as jnp
from jax import lax
from jax.experimental import pallas as pl
from jax.experimental.pallas import tpu as pltpu

BF = jnp.bfloat16
F32 = jnp.float32

D_MODEL = 2048
DEPTH = 4
HEAD_DIM = 128
N_MEM = 256
MEM_HEADS = 4
DIL_GROUPS = ((128, 1), (512, 4), (2048, 16))
A_HEADS_PER_GROUP = 8
A_BLOCK = 128
B_HEADS = 12
B_KV_GROUPS = 4
B_REP = B_HEADS // B_KV_GROUPS
CMP_LEN = 32
CMP_STRIDE = 16
CMP_HIDDEN = 512
SLC_LEN = 64
SLC_TOPK = 16
SEL_FORCE = 1e4
WIN_LEN = 512
D_FF = 5632
EPS = 1e-6

LANES = 128
VMEM_LIMIT = 56 * 1024 * 1024
NEG = -1e30
SCALE = 1.0 / math.sqrt(HEAD_DIM)
LOG2E = math.log2(math.e)
LN2 = math.log(2.0)
SCALE2 = SCALE * LOG2E
NT = (((1,), (1,)), ((), ()))


def _params(*sem):
    return pltpu.CompilerParams(dimension_semantics=sem, vmem_limit_bytes=VMEM_LIMIT)


def _masked_softmax(s, valid):
    s = jnp.where(valid, s, NEG)
    m = jnp.max(s, axis=-1, keepdims=True)
    e = jnp.where(valid, jnp.exp(s - m), 0.0)
    den = jnp.maximum(jnp.sum(e, axis=-1, keepdims=True), 1e-30)
    return e * (1.0 / den)


def _rms(x, g):
    return x * lax.rsqrt(jnp.mean(x * x, axis=-1, keepdims=True) + EPS) * g


def _iota(shape, dim):
    return lax.broadcasted_iota(jnp.int32, shape, dim)


def _rmsnorm_kernel(x_ref, g_ref, o_ref):
    o_ref[...] = _rms(x_ref[...], g_ref[...]).astype(o_ref.dtype)


def rmsnorm_rows(x, g, tm=512):
    M, D = x.shape
    tm = min(tm, M)
    return pl.pallas_call(
        _rmsnorm_kernel,
        grid=(M // tm,),
        in_specs=[pl.BlockSpec((tm, D), lambda i: (i, 0)),
                  pl.BlockSpec((1, D), lambda i: (0, 0))],
        out_specs=pl.BlockSpec((tm, D), lambda i: (i, 0)),
        out_shape=jax.ShapeDtypeStruct((M, D), BF),
        compiler_params=_params("arbitrary"),
        name="rmsnorm",
    )(x, g.reshape(1, D))


def _mm_kernel(x_ref, w_ref, o_ref, wbf_ref):
    j = pl.program_id(1)

    @pl.when(pl.program_id(0) == 0)
    def _():
        wbf_ref[j] = w_ref[...].astype(BF)

    o_ref[...] = jnp.dot(x_ref[...], wbf_ref[j],
                         preferred_element_type=F32).astype(o_ref.dtype)


def matmul(x, w, layer, *, col0=0, ncols=None, tm=1024, out_dtype=BF, name="matmul"):
    M, K = x.shape
    ncols = w.shape[2] - col0 if ncols is None else ncols
    tn = 1024 if (ncols % 1024 == 0 and col0 % 1024 == 0) else 512
    tm = min(tm, M)
    assert col0 % tn == 0 and ncols % tn == 0 and M % tm == 0
    c0 = col0 // tn
    return pl.pallas_call(
        _mm_kernel,
        grid=(M // tm, ncols // tn),
        in_specs=[pl.BlockSpec((tm, K), lambda i, j: (i, 0)),
                  pl.BlockSpec((None, K, tn), lambda i, j: (layer, 0, jnp.where(i == 0, j, 0) + c0))],
        out_specs=pl.BlockSpec((tm, tn), lambda i, j: (i, j)),
        out_shape=jax.ShapeDtypeStruct((M, ncols), out_dtype),
        scratch_shapes=[pltpu.VMEM((ncols // tn, K, tn), BF)],
        compiler_params=_params("arbitrary", "arbitrary"),
        name=name,
    )(x, w)


def _gu_kernel(x_ref, wg_ref, wu_ref, o_ref, wgb_ref, wub_ref):
    @pl.when(pl.program_id(1) == 0)
    def _():
        wgb_ref[...] = wg_ref[...].astype(BF)
        wub_ref[...] = wu_ref[...].astype(BF)

    x = x_ref[...]
    g = jnp.dot(x, wgb_ref[...], preferred_element_type=F32)
    u = jnp.dot(x, wub_ref[...], preferred_element_type=F32)
    o_ref[...] = (g * (1.0 / (1.0 + jnp.exp(-g))) * u).astype(o_ref.dtype)


def ffn_gate_up(x, w, layer, tn=512, tm=1024):
    M, K = x.shape
    nj = D_FF // tn
    return pl.pallas_call(
        _gu_kernel,
        grid=(nj, M // tm),
        in_specs=[pl.BlockSpec((tm, K), lambda j, i: (i, 0)),
                  pl.BlockSpec((None, K, tn), lambda j, i: (layer, 0, j)),
                  pl.BlockSpec((None, K, tn), lambda j, i: (layer, 0, j + nj))],
        out_specs=pl.BlockSpec((tm, tn), lambda j, i: (i, j)),
        out_shape=jax.ShapeDtypeStruct((M, D_FF), BF),
        scratch_shapes=[pltpu.VMEM((K, tn), BF), pltpu.VMEM((K, tn), BF)],
        compiler_params=_params("arbitrary", "arbitrary"),
        name="ffn_gate_up",
    )(x, w, w)


def _proj_res_kernel(nk1, nk, has_x2, has_next, *refs):
    refs = list(refs)
    x1_ref = refs.pop(0)
    x2_ref = refs.pop(0) if has_x2 else None
    w_ref, xres_ref, g_ref = refs.pop(0), refs.pop(0), refs.pop(0)
    gn_ref = refs.pop(0) if has_next else None
    xo_ref = refs.pop(0)
    uo_ref = refs.pop(0) if has_next else None
    acc_ref = refs.pop(0)
    k = pl.program_id(1)

    @pl.when(k == 0)
    def _():
        acc_ref[...] = jnp.zeros_like(acc_ref)

    @pl.when(k < nk1)
    def _():
        acc_ref[...] += jnp.dot(x1_ref[...], w_ref[...], preferred_element_type=F32)

    if has_x2:
        @pl.when(k >= nk1)
        def _():
            acc_ref[...] += jnp.dot(x2_ref[...], w_ref[...], preferred_element_type=F32)

    @pl.when(k == nk - 1)
    def _():
        xn = xres_ref[...] + _rms(acc_ref[...], g_ref[...])
        xo_ref[...] = xn
        if has_next:
            uo_ref[...] = _rms(xn, gn_ref[...]).astype(uo_ref.dtype)


def proj_residual_norm(x1, x2, x2_col0, k2, w, layer, xres, g, gn, tm=512, tk=512):
    M, K1 = x1.shape
    N = w.shape[2]
    has_x2, has_next = x2 is not None, gn is not None
    nk1 = K1 // tk
    nk = nk1 + (k2 // tk if has_x2 else 0)
    c2 = x2_col0 // tk
    in_specs = [pl.BlockSpec((tm, tk), lambda i, k: (i, jnp.minimum(k, nk1 - 1)))]
    args = [x1]
    if has_x2:
        in_specs.append(pl.BlockSpec((tm, tk), lambda i, k: (i, jnp.maximum(k - nk1, 0) + c2)))
        args.append(x2)
    in_specs += [pl.BlockSpec((None, tk, N), lambda i, k: (layer, k, 0)),
                 pl.BlockSpec((tm, N), lambda i, k: (i, 0)),
                 pl.BlockSpec((1, N), lambda i, k: (0, 0))]
    args += [w, xres, g.reshape(1, N)]
    out_specs = [pl.BlockSpec((tm, N), lambda i, k: (i, 0))]
    out_shape = [jax.ShapeDtypeStruct((M, N), F32)]
    if has_next:
        in_specs.append(pl.BlockSpec((1, N), lambda i, k: (0, 0)))
        args.append(gn.reshape(1, N))
        out_specs.append(pl.BlockSpec((tm, N), lambda i, k: (i, 0)))
        out_shape.append(jax.ShapeDtypeStruct((M, N), BF))
    res = pl.pallas_call(
        functools.partial(_proj_res_kernel, nk1, nk, has_x2, has_next),
        grid=(M // tm, nk),
        in_specs=in_specs,
        out_specs=out_specs,
        out_shape=out_shape,
        scratch_shapes=[pltpu.VMEM((tm, N), F32)],
        compiler_params=_params("arbitrary", "arbitrary"),
        name="proj_residual_norm",
    )(*args)
    return (res[0], res[1]) if has_next else (res[0], None)


def _dil_kernel(*refs):
    ng = len(DIL_GROUPS)
    q_refs, k_refs, v_refs = refs[:ng], refs[ng:2 * ng], refs[2 * ng:3 * ng]
    o_ref, o_scr, l_scr = refs[3 * ng:]
    S = o_ref.shape[1]
    blk = A_BLOCK
    n_heads = ng * A_HEADS_PER_GROUP
    j = pl.program_id(1)
    for gi, (w, d) in enumerate(DIL_GROUPS):
        q_ref, k_ref, v_ref = q_refs[gi], k_refs[gi], v_refs[gi]
        nqb = S // d // blk
        max_dist = w // d
        has_prev = nqb > 1
        nk = 2 * blk if has_prev else blk
        koff = blk if has_prev else 0
        head = (j + (gi * A_HEADS_PER_GROUP + 1)).astype(F32)
        slope = jnp.exp2(jnp.zeros((1, 1), F32) + head * (-8.0 / n_heads))
        kidx = _iota((blk, nk), 1)
        dist = _iota((blk, nk), 0) + koff - kidx
        band = (dist >= 0) & (dist <= max_dist)
        bias = jnp.where(band, (kidx - koff).astype(F32) * (slope * (d * LOG2E)), NEG)
        bias_first = jnp.where(kidx >= koff, bias, NEG)
        row_term = _iota((blk, 1), 0).astype(F32) * (slope * (-float(d)))

        def rows_at(start, d=d):
            return pl.ds(start, blk, stride=d) if d > 1 else pl.ds(start, blk)

        def body(n, carry, gi=gi, d=d, has_prev=has_prev, q_ref=q_ref, k_ref=k_ref, v_ref=v_ref,
                 bias=bias, bias_first=bias_first, row_term=row_term, rows_at=rows_at):
            r = n % d
            i = n // d
            start = r + i * (d * blk)
            rows = rows_at(start)
            q = q_ref[0, rows, :].astype(BF)
            if has_prev:
                prow = rows_at(jnp.where(i > 0, start - d * blk, start))
                k = jnp.concatenate([k_ref[0, prow, :], k_ref[0, rows, :]], axis=0).astype(BF)
                v = jnp.concatenate([v_ref[0, prow, :], v_ref[0, rows, :]], axis=0).astype(BF)
                b = jnp.where(i > 0, bias, bias_first)
            else:
                k, v, b = k_ref[0, rows, :].astype(BF), v_ref[0, rows, :].astype(BF), bias
            s = lax.dot_general(q, k, NT, preferred_element_type=F32) * SCALE2 + b
            m = jnp.max(s, axis=-1, keepdims=True)
            p = jnp.exp2(s - m)
            l = jnp.sum(p, axis=-1, keepdims=True)
            o = jnp.dot(p.astype(BF), v, preferred_element_type=F32) * (1.0 / l)
            lse = (m + jnp.log2(l)) * LN2 + row_term
            o_scr[gi, rows, :] = o
            l_scr[gi, rows, :] = jnp.broadcast_to(lse, (blk, HEAD_DIM))
            return carry

        lax.fori_loop(0, d * nqb, body, 0, unroll=8)

    tr = 256
    for c in range(S // tr):
        rows = slice(c * tr, (c + 1) * tr)
        ls = [l_scr[gi, rows, :] for gi in range(ng)]
        m = functools.reduce(jnp.maximum, ls)
        es = [jnp.exp(x - m) for x in ls]
        num = sum(e * o_scr[gi, rows, :] for gi, e in enumerate(es))
        o_ref[0, rows, :] = (num * (1.0 / sum(es))).astype(o_ref.dtype)


def dilated_attention(q, k, v):
    B, S, _ = q.shape
    ng, Hg = len(DIL_GROUPS), A_HEADS_PER_GROUP
    for w, d in DIL_GROUPS:
        assert w // d <= A_BLOCK and S % (d * A_BLOCK) == 0
    specs = [pl.BlockSpec((1, S, HEAD_DIM), lambda b, j, gi=gi: (b, 0, gi * Hg + j)) for gi in range(ng)]
    return pl.pallas_call(
        _dil_kernel,
        grid=(B, Hg),
        in_specs=specs * 3,
        out_specs=pl.BlockSpec((1, S, HEAD_DIM), lambda b, j: (b, 0, j)),
        out_shape=jax.ShapeDtypeStruct((B, S, Hg * HEAD_DIM), BF),
        scratch_shapes=[pltpu.VMEM((ng, S, HEAD_DIM), F32), pltpu.VMEM((ng, S, HEAD_DIM), F32)],
        compiler_params=_params("arbitrary", "arbitrary"),
        name="dilated_attention",
    )(*([q] * ng + [k] * ng + [v] * ng))


def _mem_attn_kernel(q_ref, kv_ref, o_ref):
    for h in range(MEM_HEADS):
        c = slice(h * HEAD_DIM, (h + 1) * HEAD_DIM)
        k = kv_ref[0, :, c]
        v = kv_ref[0, :, MEM_HEADS * HEAD_DIM + h * HEAD_DIM:MEM_HEADS * HEAD_DIM + (h + 1) * HEAD_DIM]
        s = lax.dot_general(q_ref[0, :, c], k, NT, preferred_element_type=F32) * SCALE
        m = jnp.max(s, axis=-1, keepdims=True)
        e = jnp.exp(s - m)
        p = e * (1.0 / jnp.sum(e, axis=-1, keepdims=True))
        o_ref[0, :, c] = jnp.dot(p.astype(BF), v, preferred_element_type=F32).astype(o_ref.dtype)


def memory_attention(q, q_col0, kv, tq=512):
    B, S, _ = q.shape
    w = MEM_HEADS * HEAD_DIM
    cb = q_col0 // w
    return pl.pallas_call(
        _mem_attn_kernel,
        grid=(B, S // tq),
        in_specs=[pl.BlockSpec((1, tq, w), lambda b, i: (b, i, cb)),
                  pl.BlockSpec((1, N_MEM, 2 * w), lambda b, i: (b, 0, 0))],
        out_specs=pl.BlockSpec((1, tq, w), lambda b, i: (b, i, 0)),
        out_shape=jax.ShapeDtypeStruct((B, S, w), BF),
        compiler_params=_params("arbitrary", "arbitrary"),
        name="memory_attention",
    )(q, kv)


def _gelu_tanh(x):
    return 0.5 * x * (1.0 + jnp.tanh(math.sqrt(2.0 / math.pi) * (x + 0.044715 * (x * x * x))))


def _compress_kernel(x_ref, w1_ref, w2_ref, pe_ref, o_ref, w1b_ref):
    @pl.when((pl.program_id(0) == 0) & (pl.program_id(1) == 0))
    def _():
        w1b_ref[...] = w1_ref[...].astype(BF)

    nchunk = o_ref.shape[1]
    top = jnp.zeros((nchunk, CMP_HIDDEN), F32)
    bot = jnp.zeros((nchunk, CMP_HIDDEN), F32)
    for l in range(CMP_STRIDE):
        xl = x_ref[0, pl.ds(l, nchunk, stride=CMP_STRIDE), :]
        lo, hi = l, CMP_STRIDE + l
        top += jnp.dot((xl + pe_ref[lo:lo + 1, :]).astype(BF),
                       w1b_ref[lo * HEAD_DIM:(lo + 1) * HEAD_DIM, :], preferred_element_type=F32)
        bot += jnp.dot((xl + pe_ref[hi:hi + 1, :]).astype(BF),
                       w1b_ref[hi * HEAD_DIM:(hi + 1) * HEAD_DIM, :], preferred_element_type=F32)
    hid = _gelu_tanh(top + pltpu.roll(bot, nchunk - 1, 0))
    o_ref[0] = jnp.dot(hid.astype(BF), w2_ref[...].astype(BF),
                       preferred_element_type=F32).astype(o_ref.dtype)


def compress_blocks(kvc, sec, w1, w2, pe):
    B, S, C = kvc.shape
    assert CMP_LEN == 2 * CMP_STRIDE
    G = B_KV_GROUPS
    nchunk = S // CMP_STRIDE
    return pl.pallas_call(
        _compress_kernel,
        grid=(B, G),
        in_specs=[pl.BlockSpec((1, S, HEAD_DIM), lambda b, g: (b, 0, sec * G + g)),
                  pl.BlockSpec(w1.shape, lambda b, g: (0, 0)),
                  pl.BlockSpec(w2.shape, lambda b, g: (0, 0)),
                  pl.BlockSpec(pe.shape, lambda b, g: (0, 0))],
        out_specs=pl.BlockSpec((1, nchunk, HEAD_DIM), lambda b, g: (b, 0, g)),
        out_shape=jax.ShapeDtypeStruct((B, nchunk, G * HEAD_DIM), BF),
        scratch_shapes=[pltpu.VMEM(w1.shape, BF)],
        compiler_params=_params("arbitrary", "arbitrary"),
        name=f"compress_{sec}",
    )(kvc, w1, w2, pe)


ONES_ROWS = 16
AUG_K = LANES
NEG_ROWS = 16


def _t_bf(x):
    return x.astype(F32).T.astype(BF)


def _nsa_kernel(tq, q_ref, gate_ref, kc_ref, vc_ref, ks_ref, vs_ref, kw_ref, vw_ref,
                sl_ref, o_ref, vst_ref, vwt_ref, vct_ref, neg_ref, raw_a_ref, raw_b_ref):
    S = ks_ref.shape[1]
    ncmp = kc_ref.shape[1]
    nsel = S // SLC_LEN
    i = pl.program_id(2)

    @pl.when(i == 0)
    def _():
        for src_ref, dst_ref in ((vs_ref, vst_ref), (vw_ref, vwt_ref)):
            n, _, w = dst_ref.shape
            ones = jnp.ones((ONES_ROWS, w), BF)
            for c in range(n):
                dst_ref[c] = jnp.concatenate([_t_bf(src_ref[0, c * w:(c + 1) * w, :]), ones], axis=0)
        vct_ref[...] = _t_bf(vc_ref[0])

    t0 = i * tq
    tw = B_REP * tq
    n_back = -(-(WIN_LEN - 1) // tq)
    nwin = (n_back + 1) * tq
    kcs = vst_ref.shape[2]
    bps = kcs // SLC_LEN

    def per_head(fn):
        return jnp.concatenate([fn(r) for r in range(B_REP)], axis=1)

    def bf_exact(x):
        return x.astype(BF).astype(F32)

    q_all = per_head(lambda r: _t_bf(q_ref[0, :, r * HEAD_DIM:(r + 1) * HEAD_DIM]))
    sl_row = per_head(lambda r: jnp.broadcast_to(sl_ref[0, 0:1, r:r + 1], (1, tq)))
    sl2_row = sl_row * LOG2E
    t_q = t0 + _iota((1, tq), 1)
    t_row = per_head(lambda r: t_q)

    c1 = bf_exact(sl_row * (1.0 / SCALE))
    c2 = bf_exact(sl_row * (1.0 / SCALE) - c1)
    c3 = bf_exact(sl_row * (1.0 / SCALE) - c1 - c2)
    slope_rows = jnp.concatenate([c1, c1, c2, c2, c3, c3, jnp.zeros((AUG_K - NEG_ROWS - 6, tw), F32)],
                                 axis=0).astype(BF)

    def key_aug(nk):
        idx = _iota((nk, AUG_K), 0)
        lane = _iota((nk, AUG_K), 1)
        piece = jnp.where((lane - NEG_ROWS) % 2 == 0, (idx // 256) * 256, idx % 256)
        val = jnp.where(lane < bps, (idx // SLC_LEN == lane).astype(jnp.int32),
                        jnp.where((lane >= NEG_ROWS) & (lane < NEG_ROWS + 6), piece, 0))
        return val.astype(F32).astype(BF)

    def scores(k_c, k_aug, neg_rows):
        q_full = jnp.concatenate([q_all, neg_rows, slope_rows], axis=0)
        return jnp.dot(jnp.concatenate([k_c, k_aug], axis=1), q_full, preferred_element_type=F32)

    def neg_rows_of(n):
        rows = neg_ref[pl.ds(pl.multiple_of(n * bps, bps), bps), :]
        return jnp.concatenate([per_head(lambda r: rows), jnp.zeros((NEG_ROWS - bps, tw), F32)], axis=0).astype(BF)

    jj = _iota((nsel, ncmp), 0) * SLC_LEN
    cc = _iota((nsel, ncmp), 1) * CMP_STRIDE
    ov_t = jnp.maximum(jnp.minimum(cc + CMP_LEN, jj + SLC_LEN) - jnp.maximum(cc, jj), 0).astype(F32) / CMP_LEN

    dist_c = t_row - (_iota((ncmp, 1), 0) * CMP_STRIDE + (CMP_LEN - 1))
    s = (jnp.dot(kc_ref[0], q_all, preferred_element_type=F32) * SCALE2
         + jnp.where(dist_c >= 0, dist_c.astype(F32) * (-sl2_row), NEG))
    e = jnp.exp2(s - jnp.max(s, axis=0, keepdims=True))
    has_block = (t_row >= CMP_LEN - 1).astype(F32)
    p = e * (has_block / jnp.maximum(jnp.sum(e, axis=0, keepdims=True), 1e-30))
    o_cmp = jnp.dot(vct_ref[...], p.astype(BF), preferred_element_type=F32)
    imp_h = jnp.dot(ov_t, p, preferred_element_type=F32, precision=lax.Precision.HIGHEST)
    imp = sum(imp_h[:, r * tq:(r + 1) * tq] for r in range(B_REP))

    jb = _iota((nsel, tq), 0)
    cur = t_q // SLC_LEN
    forced = (jb == 0) | (jb == cur) | (jb == cur - 1)
    imp = jnp.where(forced, SEL_FORCE, jnp.where(jb > cur, -SEL_FORCE, imp))
    beaten_by = jnp.zeros((nsel, tq), jnp.int32)
    for j in range(nsel):
        row = imp[j:j + 1, :]
        beats = (row > imp) | ((row == imp) & (jb > j))
        beaten_by = beaten_by + beats.astype(jnp.int32)
    neg_ref[...] = jnp.where(beaten_by >= min(SLC_TOPK, nsel), NEG, 0.0)

    def softmax_step(raw, vt_c, mask, c_off, st):
        m, acc = st
        s = raw * SCALE2
        if mask is not None:
            s = s + per_head(lambda r: mask)
        m_new = jnp.maximum(m, jnp.max(s, axis=0, keepdims=True) + c_off)
        p = jnp.exp2(s - (m_new - c_off))
        acc = jnp.exp2(m - m_new) * acc + jnp.dot(vt_c, p.astype(BF), preferred_element_type=F32)
        return m_new, acc

    def finish(st):
        _, acc = st
        return acc[:HEAD_DIM] * (1.0 / acc[HEAD_DIM:HEAD_DIM + 1])

    init = (jnp.full((1, tw), NEG, F32), jnp.zeros((HEAD_DIM + ONES_ROWS, tw), F32))
    no_off = jnp.zeros((1, 1), F32)

    c0 = jnp.maximum(i - n_back, 0)
    wbase = pl.multiple_of(c0 * tq, tq)
    dist = t_q - (wbase + _iota((nwin, tq), 0))
    wmask = jnp.where((dist >= 0) & (dist <= WIN_LEN - 1), 0.0, NEG)
    vwt = jnp.concatenate([vwt_ref[c0 + j] for j in range(n_back + 1)], axis=1)
    raw_w = scores(kw_ref[0, pl.ds(wbase, nwin), :], key_aug(nwin), jnp.zeros((NEG_ROWS, tw), BF))
    o_win = finish(softmax_step(raw_w, vwt, wmask, no_off, init))

    k_aug = key_aug(kcs)
    n_last = (t0 + tq - 1) // kcs

    def slab_scores(n):
        return scores(ks_ref[0, pl.ds(pl.multiple_of(n * kcs, kcs), kcs), :], k_aug, neg_rows_of(n))

    causal = jnp.where(n_last * kcs + _iota((kcs, tq), 0) <= t_q, 0.0, NEG)
    st = softmax_step(slab_scores(n_last), vst_ref[n_last], causal, (n_last * kcs - t0).astype(F32) * sl2_row, init)

    def sel_slab(n, st, src_ref, dst_ref):
        dst_ref[...] = slab_scores(jnp.minimum(n + 1, n_last))
        return softmax_step(src_ref[...], vst_ref[n], None, (n * kcs - t0).astype(F32) * sl2_row, st)

    def sel_pair(j, st):
        st = sel_slab(2 * j, st, raw_a_ref, raw_b_ref)
        return lax.cond(2 * j + 1 < n_last, lambda s: sel_slab(2 * j + 1, s, raw_b_ref, raw_a_ref),
                        lambda s: s, st)

    raw_a_ref[...] = slab_scores(0)
    o_slc = finish(lax.fori_loop(0, (n_last + 1) // 2, sel_pair, st))

    gate_t = (1.0 / (1.0 + jnp.exp(-gate_ref[0]))).T
    for r in range(B_REP):
        c = slice(r * tq, (r + 1) * tq)
        o_t = (gate_t[3 * r:3 * r + 1] * o_cmp[:, c] + gate_t[3 * r + 1:3 * r + 2] * o_slc[:, c]
               + gate_t[3 * r + 2:3 * r + 3] * o_win[:, c])
        o_ref[0, :, r * HEAD_DIM:(r + 1) * HEAD_DIM] = o_t.T.astype(o_ref.dtype)


def nsa_attention(qp, gates, kc, vc, kv, tq=256):
    B, S, _ = qp.shape
    G = B_KV_GROUPS
    gq = B_REP * HEAD_DIM
    ncmp = kc.shape[1]
    kcs = 2 * tq
    nsel = S // SLC_LEN
    assert kcs % SLC_LEN == 0 and S % kcs == 0 and S >= (-(-(WIN_LEN - 1) // tq) + 1) * tq
    slopes = 2.0 ** (-8.0 * jnp.arange(1, B_HEADS + 1, dtype=F32) / B_HEADS)
    sl = jnp.zeros((G, 8, LANES), F32).at[:, :, :B_REP].set(
        jnp.broadcast_to(slopes.reshape(G, 1, B_REP), (G, 8, B_REP)))
    kvspec = lambda sec: pl.BlockSpec((1, S, HEAD_DIM), lambda b, g, i: (b, 0, sec * G + g))
    cspec = pl.BlockSpec((1, ncmp, HEAD_DIM), lambda b, g, i: (b, 0, g))
    vt_rows = HEAD_DIM + ONES_ROWS
    return pl.pallas_call(
        functools.partial(_nsa_kernel, tq),
        grid=(B, G, S // tq),
        in_specs=[pl.BlockSpec((1, tq, gq), lambda b, g, i: (b, i, g)),
                  pl.BlockSpec((1, tq, LANES), lambda b, g, i: (b, i, g)),
                  cspec, cspec, kvspec(0), kvspec(1), kvspec(2), kvspec(3),
                  pl.BlockSpec((1, 8, LANES), lambda b, g, i: (g, 0, 0))],
        out_specs=pl.BlockSpec((1, tq, gq), lambda b, g, i: (b, i, g)),
        out_shape=jax.ShapeDtypeStruct((B, S, B_HEADS * HEAD_DIM), BF),
        scratch_shapes=[pltpu.VMEM((S // kcs, vt_rows, kcs), BF), pltpu.VMEM((S // tq, vt_rows, tq), BF),
                        pltpu.VMEM((HEAD_DIM, ncmp), BF), pltpu.VMEM((nsel, tq), F32),
                        pltpu.VMEM((kcs, B_REP * tq), F32), pltpu.VMEM((kcs, B_REP * tq), F32)],
        compiler_params=_params("arbitrary", "arbitrary", "arbitrary"),
        name="nsa_attention",
    )(qp, gates, kc, vc, kv, kv, kv, kv, sl)


def kernel(x, mem, norm_g, a_w_in, a_w_out, b_w_in, b_w_out, mem_w_kv, ffn_w_gu, ffn_w_down,
           kv_norm_g, kv_w, cmp_pe, cmp_wk1, cmp_wk2, cmp_wv1, cmp_wv2):
    B, S, D = x.shape
    M = B * S
    n_a = DEPTH // 2
    qa_cols = len(DIL_GROUPS) * A_HEADS_PER_GROUP * HEAD_DIM
    qb_cols = B_HEADS * HEAD_DIM
    mq_cols = MEM_HEADS * HEAD_DIM
    n_gate = 3 * B_HEADS
    G = B_KV_GROUPS

    b_w_qm = jnp.concatenate([b_w_in[:, :, :qb_cols], b_w_in[:, :, qb_cols + n_gate:]], axis=-1)
    b_w_gate = jnp.pad(b_w_in[:, :, qb_cols:qb_cols + n_gate].reshape(-1, D, G, n_gate // G),
                       ((0, 0), (0, 0), (0, 0), (0, LANES - n_gate // G))).reshape(-1, D, G * LANES)
    a_w_out_b = a_w_out.astype(BF)
    b_w_out_b = b_w_out.astype(BF)
    w_down_b = ffn_w_down.astype(BF)
    kv_w3 = kv_w[None]

    x2 = x.reshape(M, D)
    mem2 = mem.reshape(B * N_MEM, D)
    u = rmsnorm_rows(x2, norm_g[0, 0])
    shared = None
    for l in range(DEPTH):
        g = norm_g[l]
        mem_n = rmsnorm_rows(mem2, g[4])
        mkv = matmul(mem_n, mem_w_kv, l, name="mem_kv").reshape(B, N_MEM, 2 * mq_cols)
        if l < n_a:
            q = matmul(u, a_w_in, l, col0=0, ncols=qa_cols, out_dtype=F32, name="a_q").reshape(B, S, qa_cols)
            k = matmul(u, a_w_in, l, col0=qa_cols, ncols=qa_cols, out_dtype=F32, name="a_k").reshape(B, S, qa_cols)
            v = matmul(u, a_w_in, l, col0=2 * qa_cols, ncols=qa_cols, out_dtype=F32,
                       name="a_v").reshape(B, S, qa_cols)
            mq = matmul(u, a_w_in, l, col0=3 * qa_cols, ncols=mq_cols, name="a_mq").reshape(B, S, mq_cols)
            o_main = dilated_attention(q, k, v)
            o_mem = memory_attention(mq, 0, mkv)
            w_out = a_w_out_b
            li = l
        else:
            li = l - n_a
            if shared is None:
                kvn = rmsnorm_rows(x2, kv_norm_g)
                kvc = matmul(kvn, kv_w3, 0, col0=0, ncols=2 * G * HEAD_DIM, out_dtype=F32,
                             name="kv_cmp").reshape(B, S, 2 * G * HEAD_DIM)
                kvr = matmul(kvn, kv_w3, 0, col0=2 * G * HEAD_DIM, ncols=4 * G * HEAD_DIM,
                             name="kv_rest").reshape(B, S, 4 * G * HEAD_DIM)
                kc = compress_blocks(kvc, 0, cmp_wk1, cmp_wk2, cmp_pe[0])
                vc = compress_blocks(kvc, 1, cmp_wv1, cmp_wv2, cmp_pe[1])
                shared = (kc, vc, kvr)
            qm = matmul(u, b_w_qm, li, name="b_qm").reshape(B, S, qb_cols + mq_cols)
            gates = matmul(u, b_w_gate, li, out_dtype=F32, name="b_gate").reshape(B, S, G * LANES)
            o_main = nsa_attention(qm, gates, *shared)
            o_mem = memory_attention(qm, qb_cols, mkv)
            w_out = b_w_out_b
        x2, u = proj_residual_norm(o_main.reshape(M, -1), o_mem.reshape(M, mq_cols), 0, mq_cols,
                                   w_out, li, x2, g[1], g[2])
        h = ffn_gate_up(u, ffn_w_gu, l)
        gn = norm_g[l + 1, 0] if l + 1 < DEPTH else None
        x2, u = proj_residual_norm(h, None, 0, 0, w_down_b, l, x2, g[3], gn, tk=D_FF // 4)
    return x2.reshape(B, S, D)
```

```python
import functools
import math

import jax
import jax.numpy as jnp
from jax import lax
from jax.experimental import pallas as pl
from jax.experimental.pallas import tpu as pltpu

BF = jnp.bfloat16
F32 = jnp.float32

D_MODEL = 2048
DEPTH = 4
HEAD_DIM = 128
N_MEM = 256
MEM_HEADS = 4
DIL_GROUPS = ((128, 1), (512, 4), (2048, 16))
A_HEADS_PER_GROUP = 8
A_BLOCK = 128
B_HEADS = 12
B_KV_GROUPS = 4
B_REP = B_HEADS // B_KV_GROUPS
CMP_LEN = 32
CMP_STRIDE = 16
CMP_HIDDEN = 512
SLC_LEN = 64
SLC_TOPK = 16
SEL_FORCE = 1e4
WIN_LEN = 512
D_FF = 5632
EPS = 1e-6

LANES = 128
VMEM_LIMIT = 56 * 1024 * 1024
NEG = -1e30
SCALE = 1.0 / math.sqrt(HEAD_DIM)
LOG2E = math.log2(math.e)
LN2 = math.log(2.0)
SCALE2 = SCALE * LOG2E
NT = (((1,), (1,)), ((), ()))
DIL_GROUP_BLOCKS = 8
STAGE_STRIDE = 4


def _params(*sem):
    return pltpu.CompilerParams(dimension_semantics=sem, vmem_limit_bytes=VMEM_LIMIT)


def _masked_softmax(s, valid):
    s = jnp.where(valid, s, NEG)
    m = jnp.max(s, axis=-1, keepdims=True)
    e = jnp.where(valid, jnp.exp(s - m), 0.0)
    den = jnp.maximum(jnp.sum(e, axis=-1, keepdims=True), 1e-30)
    return e * (1.0 / den)


def _rms(x, g):
    return x * lax.rsqrt(jnp.mean(x * x, axis=-1, keepdims=True) + EPS) * g


def _iota(shape, dim):
    return lax.broadcasted_iota(jnp.int32, shape, dim)


def _rmsnorm_kernel(x_ref, g_ref, o_ref):
    o_ref[...] = _rms(x_ref[...], g_ref[...]).astype(o_ref.dtype)


def rmsnorm_rows(x, g, tm=512):
    M, D = x.shape
    tm = min(tm, M)
    return pl.pallas_call(
        _rmsnorm_kernel,
        grid=(M // tm,),
        in_specs=[pl.BlockSpec((tm, D), lambda i: (i, 0)),
                  pl.BlockSpec((1, D), lambda i: (0, 0))],
        out_specs=pl.BlockSpec((tm, D), lambda i: (i, 0)),
        out_shape=jax.ShapeDtypeStruct((M, D), BF),
        compiler_params=_params("arbitrary"),
        name="rmsnorm",
    )(x, g.reshape(1, D))


def _mm_kernel(x_ref, w_ref, o_ref, wbf_ref):
    j = pl.program_id(1)

    @pl.when(pl.program_id(0) == 0)
    def _():
        wbf_ref[j] = w_ref[...].astype(BF)

    o_ref[...] = jnp.dot(x_ref[...], wbf_ref[j],
                         preferred_element_type=F32).astype(o_ref.dtype)


def matmul(x, w, layer, *, col0=0, ncols=None, tm=1024, out_dtype=BF, name="matmul"):
    M, K = x.shape
    ncols = w.shape[2] - col0 if ncols is None else ncols
    tn = 1024 if (ncols % 1024 == 0 and col0 % 1024 == 0) else 512
    tm = min(tm, M)
    assert col0 % tn == 0 and ncols % tn == 0 and M % tm == 0
    c0 = col0 // tn
    return pl.pallas_call(
        _mm_kernel,
        grid=(M // tm, ncols // tn),
        in_specs=[pl.BlockSpec((tm, K), lambda i, j: (i, 0)),
                  pl.BlockSpec((None, K, tn), lambda i, j: (layer, 0, jnp.where(i == 0, j, 0) + c0))],
        out_specs=pl.BlockSpec((tm, tn), lambda i, j: (i, j)),
        out_shape=jax.ShapeDtypeStruct((M, ncols), out_dtype),
        scratch_shapes=[pltpu.VMEM((ncols // tn, K, tn), BF)],
        compiler_params=_params("arbitrary", "arbitrary"),
        name=name,
    )(x, w)


def _gu_kernel(x_ref, wg_ref, wu_ref, o_ref, wgb_ref, wub_ref):
    @pl.when(pl.program_id(1) == 0)
    def _():
        wgb_ref[...] = wg_ref[...].astype(BF)
        wub_ref[...] = wu_ref[...].astype(BF)

    x = x_ref[...]
    g = jnp.dot(x, wgb_ref[...], preferred_element_type=F32)
    u = jnp.dot(x, wub_ref[...], preferred_element_type=F32)
    o_ref[...] = (g * (1.0 / (1.0 + jnp.exp(-g))) * u).astype(o_ref.dtype)


def ffn_gate_up(x, w, layer, tn=512, tm=1024):
    M, K = x.shape
    nj = D_FF // tn
    return pl.pallas_call(
        _gu_kernel,
        grid=(nj, M // tm),
        in_specs=[pl.BlockSpec((tm, K), lambda j, i: (i, 0)),
                  pl.BlockSpec((None, K, tn), lambda j, i: (layer, 0, j)),
                  pl.BlockSpec((None, K, tn), lambda j, i: (layer, 0, j + nj))],
        out_specs=pl.BlockSpec((tm, tn), lambda j, i: (i, j)),
        out_shape=jax.ShapeDtypeStruct((M, D_FF), BF),
        scratch_shapes=[pltpu.VMEM((K, tn), BF), pltpu.VMEM((K, tn), BF)],
        compiler_params=_params("arbitrary", "arbitrary"),
        name="ffn_gate_up",
    )(x, w, w)


def _proj_res_kernel(has_x2, has_next, n_sub, *refs):
    refs = list(refs)
    x1_ref = refs.pop(0)
    x2_ref = refs.pop(0) if has_x2 else None
    w_ref, xres_ref, g_ref = refs.pop(0), refs.pop(0), refs.pop(0)
    gn_ref = refs.pop(0) if has_next else None
    xo_ref = refs.pop(0)
    uo_ref = refs.pop(0) if has_next else None
    k1 = x1_ref.shape[1]
    sub = xo_ref.shape[0] // n_sub
    for h in range(n_sub):
        rows = slice(h * sub, (h + 1) * sub)
        acc = jnp.dot(x1_ref[rows, :], w_ref[:k1, :], preferred_element_type=F32)
        if has_x2:
            acc = acc + jnp.dot(x2_ref[rows, :], w_ref[k1:, :], preferred_element_type=F32)
        xn = xres_ref[rows, :] + _rms(acc, g_ref[...])
        xo_ref[rows, :] = xn
        if has_next:
            uo_ref[rows, :] = _rms(xn, gn_ref[...]).astype(uo_ref.dtype)


def proj_residual_norm(x1, x2, x2_col0, k2, w, layer, xres, g, gn, tm=512, n_sub=2):
    M, K1 = x1.shape
    K, N = w.shape[1], w.shape[2]
    has_x2, has_next = x2 is not None, gn is not None
    assert K == K1 + (k2 if has_x2 else 0)
    row = lambda i: (i, 0)
    in_specs = [pl.BlockSpec((tm, K1), row)]
    args = [x1]
    if has_x2:
        assert x2_col0 % k2 == 0
        in_specs.append(pl.BlockSpec((tm, k2), lambda i: (i, x2_col0 // k2)))
        args.append(x2)
    in_specs += [pl.BlockSpec((None, K, N), lambda i: (layer, 0, 0), pipeline_mode=pl.Buffered(1)),
                 pl.BlockSpec((tm, N), row),
                 pl.BlockSpec((1, N), lambda i: (0, 0))]
    args += [w, xres, g.reshape(1, N)]
    out_specs = [pl.BlockSpec((tm, N), row)]
    out_shape = [jax.ShapeDtypeStruct((M, N), F32)]
    if has_next:
        in_specs.append(pl.BlockSpec((1, N), lambda i: (0, 0)))
        args.append(gn.reshape(1, N))
        out_specs.append(pl.BlockSpec((tm, N), row))
        out_shape.append(jax.ShapeDtypeStruct((M, N), BF))
    res = pl.pallas_call(
        functools.partial(_proj_res_kernel, has_x2, has_next, n_sub),
        grid=(M // tm,),
        in_specs=in_specs,
        out_specs=out_specs,
        out_shape=out_shape,
        compiler_params=_params("arbitrary"),
        name="proj_residual_norm",
    )(*args)
    return (res[0], res[1]) if has_next else (res[0], None)


def _divmod_pow2(n, d):
    assert d & (d - 1) == 0
    return lax.shift_right_logical(n, d.bit_length() - 1), n & (d - 1)


def _dil_kernel(*refs):
    ng = len(DIL_GROUPS)
    q_refs, k_refs, v_refs = refs[:ng], refs[ng:2 * ng], refs[2 * ng:3 * ng]
    o_ref, o_scr, l_scr, stage_scr, ostage_scr = refs[3 * ng:]
    S = o_ref.shape[1]
    blk = A_BLOCK
    n_heads = ng * A_HEADS_PER_GROUP
    j = pl.program_id(1)
    for gi, (w, d) in enumerate(DIL_GROUPS):
        q_ref, k_ref, v_ref = q_refs[gi], k_refs[gi], v_refs[gi]
        nqb = S // d // blk
        max_dist = w // d
        has_prev = nqb > 1
        nk = 2 * blk if has_prev else blk
        koff = blk if has_prev else 0
        head = (j + (gi * A_HEADS_PER_GROUP + 1)).astype(F32)
        slope = jnp.exp2(jnp.zeros((1, 1), F32) + head * (-8.0 / n_heads))
        kidx = _iota((blk, nk), 1)
        dist = _iota((blk, nk), 0) + koff - kidx
        band = (dist >= 0) & (dist <= max_dist)
        bias = jnp.where(band, (kidx - koff).astype(F32) * (slope * (d * LOG2E)), NEG)
        bias_first = jnp.where(kidx >= koff, bias, NEG)
        row_term = _iota((blk, 1), 0).astype(F32) * (slope * (-float(d)))

        staged = d > STAGE_STRIDE
        if staged:
            assert d % STAGE_STRIDE == 0 and d // STAGE_STRIDE <= STAGE_STRIDE
            quarter = S // STAGE_STRIDE
            for t_idx, ref in enumerate((q_ref, k_ref, v_ref)):
                for r1 in range(STAGE_STRIDE):
                    stage_scr[t_idx, r1 * quarter:(r1 + 1) * quarter, :] = ref[0, pl.ds(r1, quarter, stride=STAGE_STRIDE), :]

        def out_rows(r, i, d=d):
            start = r + i * (d * blk)
            return pl.ds(start, blk, stride=d) if d > 1 else pl.ds(start, blk)

        def staged_rows(r, i, d=d):
            s2 = d // STAGE_STRIDE
            r2, r1 = _divmod_pow2(r, STAGE_STRIDE)
            return pl.ds(r1 * (S // STAGE_STRIDE) + r2 + i * (s2 * blk), blk, stride=s2)

        def load(t_idx, ref, r, i, staged=staged):
            return stage_scr[t_idx, staged_rows(r, i), :] if staged else ref[0, out_rows(r, i), :]

        def body(n0, carry, gi=gi, d=d, has_prev=has_prev, q_ref=q_ref, k_ref=k_ref, v_ref=v_ref,
                 bias=bias, bias_first=bias_first, row_term=row_term, out_rows=out_rows, load=load,
                 staged=staged, staged_rows=staged_rows):
            ri = [_divmod_pow2(n0 * DIL_GROUP_BLOCKS + g, d)[::-1] for g in range(DIL_GROUP_BLOCKS)]
            qs = [load(0, q_ref, r, i).astype(BF) for r, i in ri]
            if has_prev:
                ips = [jnp.maximum(i - 1, 0) for _, i in ri]
                ks = [jnp.concatenate([load(1, k_ref, r, ip), load(1, k_ref, r, i)], axis=0).astype(BF)
                      for (r, i), ip in zip(ri, ips)]
                vs = [jnp.concatenate([load(2, v_ref, r, ip), load(2, v_ref, r, i)], axis=0).astype(BF)
                      for (r, i), ip in zip(ri, ips)]
                bs = [jnp.where(i > 0, bias, bias_first) for _, i in ri]
            else:
                ks = [load(1, k_ref, r, i).astype(BF) for r, i in ri]
                vs = [load(2, v_ref, r, i).astype(BF) for r, i in ri]
                bs = [bias] * len(ri)
            ss = [lax.dot_general(q, k, NT, preferred_element_type=F32) * SCALE2 + b for q, k, b in zip(qs, ks, bs)]
            ms = [jnp.max(s, axis=-1, keepdims=True) for s in ss]
            ps = [jnp.exp2(s - m) for s, m in zip(ss, ms)]
            ls = [jnp.sum(p, axis=-1, keepdims=True) for p in ps]
            os_ = [jnp.dot(p.astype(BF), v, preferred_element_type=F32) * (1.0 / l) for p, v, l in zip(ps, vs, ls)]
            for (r, i), o, m, l in zip(ri, os_, ms, ls):
                lse_b = jnp.broadcast_to((m + jnp.log2(l)) * LN2 + row_term, (blk, HEAD_DIM))
                if staged:
                    rows = staged_rows(r, i)
                    ostage_scr[0, rows, :] = o
                    ostage_scr[1, rows, :] = lse_b
                else:
                    rows = out_rows(r, i)
                    o_scr[gi, rows, :] = o
                    l_scr[gi, rows, :] = lse_b
            return carry

        assert (d * nqb) % DIL_GROUP_BLOCKS == 0
        lax.fori_loop(0, d * nqb // DIL_GROUP_BLOCKS, body, 0)
        if staged:
            for r1 in range(STAGE_STRIDE):
                dst = pl.ds(r1, quarter, stride=STAGE_STRIDE)
                o_scr[gi, dst, :] = ostage_scr[0, r1 * quarter:(r1 + 1) * quarter, :]
                l_scr[gi, dst, :] = ostage_scr[1, r1 * quarter:(r1 + 1) * quarter, :]

    tr = 256
    for c in range(S // tr):
        rows = slice(c * tr, (c + 1) * tr)
        ls = [l_scr[gi, rows, :] for gi in range(ng)]
        m = functools.reduce(jnp.maximum, ls)
        es = [jnp.exp(x - m) for x in ls]
        num = sum(e * o_scr[gi, rows, :] for gi, e in enumerate(es))
        o_ref[0, rows, :] = (num * (1.0 / sum(es))).astype(o_ref.dtype)


def dilated_attention(q, k, v):
    B, S, _ = q.shape
    ng, Hg = len(DIL_GROUPS), A_HEADS_PER_GROUP
    for w, d in DIL_GROUPS:
        assert w // d <= A_BLOCK and S % (d * A_BLOCK) == 0
    specs = [pl.BlockSpec((1, S, HEAD_DIM), lambda b, j, gi=gi: (b, 0, gi * Hg + j)) for gi in range(ng)]
    return pl.pallas_call(
        _dil_kernel,
        grid=(B, Hg),
        in_specs=specs * 3,
        out_specs=pl.BlockSpec((1, S, HEAD_DIM), lambda b, j: (b, 0, j)),
        out_shape=jax.ShapeDtypeStruct((B, S, Hg * HEAD_DIM), BF),
        scratch_shapes=[pltpu.VMEM((ng, S, HEAD_DIM), F32), pltpu.VMEM((ng, S, HEAD_DIM), F32),
                        pltpu.VMEM((3, S, HEAD_DIM), F32), pltpu.VMEM((2, S, HEAD_DIM), F32)],
        compiler_params=_params("arbitrary", "arbitrary"),
        name="dilated_attention",
    )(*([q] * ng + [k] * ng + [v] * ng))


def _mem_attn_kernel(q_ref, kv_ref, o_ref):
    for h in range(MEM_HEADS):
        c = slice(h * HEAD_DIM, (h + 1) * HEAD_DIM)
        k = kv_ref[0, :, c]
        v = kv_ref[0, :, MEM_HEADS * HEAD_DIM + h * HEAD_DIM:MEM_HEADS * HEAD_DIM + (h + 1) * HEAD_DIM]
        s = lax.dot_general(q_ref[0, :, c], k, NT, preferred_element_type=F32) * SCALE
        m = jnp.max(s, axis=-1, keepdims=True)
        e = jnp.exp(s - m)
        p = e * (1.0 / jnp.sum(e, axis=-1, keepdims=True))
        o_ref[0, :, c] = jnp.dot(p.astype(BF), v, preferred_element_type=F32).astype(o_ref.dtype)


def memory_attention(q, q_col0, kv, tq=512):
    B, S, _ = q.shape
    w = MEM_HEADS * HEAD_DIM
    cb = q_col0 // w
    return pl.pallas_call(
        _mem_attn_kernel,
        grid=(B, S // tq),
        in_specs=[pl.BlockSpec((1, tq, w), lambda b, i: (b, i, cb)),
                  pl.BlockSpec((1, N_MEM, 2 * w), lambda b, i: (b, 0, 0))],
        out_specs=pl.BlockSpec((1, tq, w), lambda b, i: (b, i, 0)),
        out_shape=jax.ShapeDtypeStruct((B, S, w), BF),
        compiler_params=_params("arbitrary", "arbitrary"),
        name="memory_attention",
    )(q, kv)


def _gelu_tanh(x):
    return 0.5 * x * (1.0 + jnp.tanh(math.sqrt(2.0 / math.pi) * (x + 0.044715 * (x * x * x))))


def _compress_kernel(x_ref, w1_ref, w2_ref, pe_ref, o_ref, w1b_ref):
    @pl.when((pl.program_id(0) == 0) & (pl.program_id(1) == 0))
    def _():
        w1b_ref[...] = w1_ref[...].astype(BF)

    nchunk = o_ref.shape[1]
    top = jnp.zeros((nchunk, CMP_HIDDEN), F32)
    bot = jnp.zeros((nchunk, CMP_HIDDEN), F32)
    for l in range(CMP_STRIDE):
        xl = x_ref[0, pl.ds(l, nchunk, stride=CMP_STRIDE), :]
        lo, hi = l, CMP_STRIDE + l
        top += jnp.dot((xl + pe_ref[lo:lo + 1, :]).astype(BF),
                       w1b_ref[lo * HEAD_DIM:(lo + 1) * HEAD_DIM, :], preferred_element_type=F32)
        bot += jnp.dot((xl + pe_ref[hi:hi + 1, :]).astype(BF),
                       w1b_ref[hi * HEAD_DIM:(hi + 1) * HEAD_DIM, :], preferred_element_type=F32)
    hid = _gelu_tanh(top + pltpu.roll(bot, nchunk - 1, 0))
    o_ref[0] = jnp.dot(hid.astype(BF), w2_ref[...].astype(BF),
                       preferred_element_type=F32).astype(o_ref.dtype)


def compress_blocks(kvc, sec, w1, w2, pe):
    B, S, C = kvc.shape
    assert CMP_LEN == 2 * CMP_STRIDE
    G = B_KV_GROUPS
    nchunk = S // CMP_STRIDE
    return pl.pallas_call(
        _compress_kernel,
        grid=(B, G),
        in_specs=[pl.BlockSpec((1, S, HEAD_DIM), lambda b, g: (b, 0, sec * G + g)),
                  pl.BlockSpec(w1.shape, lambda b, g: (0, 0)),
                  pl.BlockSpec(w2.shape, lambda b, g: (0, 0)),
                  pl.BlockSpec(pe.shape, lambda b, g: (0, 0))],
        out_specs=pl.BlockSpec((1, nchunk, HEAD_DIM), lambda b, g: (b, 0, g)),
        out_shape=jax.ShapeDtypeStruct((B, nchunk, G * HEAD_DIM), BF),
        scratch_shapes=[pltpu.VMEM(w1.shape, BF)],
        compiler_params=_params("arbitrary", "arbitrary"),
        name=f"compress_{sec}",
    )(kvc, w1, w2, pe)


ONES_ROWS = 16
AUG_K = LANES
NEG_ROWS = 16


def _key_aug(nk, bps):
    idx = _iota((nk, AUG_K), 0)
    lane = _iota((nk, AUG_K), 1)
    piece = jnp.where((lane - NEG_ROWS) % 2 == 0, (idx // 256) * 256, idx % 256)
    val = jnp.where(lane < bps, (idx // SLC_LEN == lane).astype(jnp.int32),
                    jnp.where((lane >= NEG_ROWS) & (lane < NEG_ROWS + 6), piece, 0))
    return val.astype(F32).astype(BF)


def _offset_masks(n, nk, tq, lo, hi):
    dist = (_iota((n, nk, tq), 0) * tq + _iota((n, nk, tq), 2)) - _iota((n, nk, tq), 1)
    return jnp.where((dist >= lo) & (dist <= hi), 0.0, NEG).astype(F32)


def _t_bf(x):
    return x.astype(F32).T.astype(BF)


def _nsa_kernel(tq, q_ref, gate_ref, kc_ref, vc_ref, ks_ref, vs_ref, kw_ref, vw_ref,
                sl_ref, kaug_w_ref, kaug_s_ref, wmask_ref, cmask_ref, o_ref,
                vst_ref, vwt_ref, vct_ref, neg_ref, raw_a_ref, raw_b_ref, pc_ref):
    S = ks_ref.shape[1]
    ncmp = kc_ref.shape[1]
    nsel = S // SLC_LEN
    i = pl.program_id(2)

    @pl.when(i == 0)
    def _():
        for src_ref, dst_ref in ((vs_ref, vst_ref), (vw_ref, vwt_ref)):
            n, _, w = dst_ref.shape
            ones = jnp.ones((ONES_ROWS, w), BF)
            for c in range(n):
                dst_ref[c] = jnp.concatenate([_t_bf(src_ref[0, c * w:(c + 1) * w, :]), ones], axis=0)
        vct_ref[...] = _t_bf(vc_ref[0])

    t0 = i * tq
    tw = B_REP * tq
    n_back = -(-(WIN_LEN - 1) // tq)
    nwin = (n_back + 1) * tq
    kcs = vst_ref.shape[2]
    bps = kcs // SLC_LEN

    def per_head(fn):
        return jnp.concatenate([fn(r) for r in range(B_REP)], axis=1)

    def bf_exact(x):
        return x.astype(BF).astype(F32)

    q_all = per_head(lambda r: _t_bf(q_ref[0, :, r * HEAD_DIM:(r + 1) * HEAD_DIM]))
    sl_row = per_head(lambda r: jnp.broadcast_to(sl_ref[0, 0:1, r:r + 1], (1, tq)))
    sl2_row = sl_row * LOG2E
    t_q = t0 + _iota((1, tq), 1)
    t_row = per_head(lambda r: t_q)

    c1 = bf_exact(sl_row * (1.0 / SCALE))
    c2 = bf_exact(sl_row * (1.0 / SCALE) - c1)
    c3 = bf_exact(sl_row * (1.0 / SCALE) - c1 - c2)
    slope_rows = jnp.concatenate([c1, c1, c2, c2, c3, c3, jnp.zeros((AUG_K - NEG_ROWS - 6, tw), F32)],
                                 axis=0).astype(BF)

    def scores(k_c, k_aug, neg_rows):
        q_full = jnp.concatenate([q_all, neg_rows, slope_rows], axis=0)
        return jnp.dot(jnp.concatenate([k_c, k_aug], axis=1), q_full, preferred_element_type=F32)

    def neg_rows_of(n):
        rows = neg_ref[pl.ds(pl.multiple_of(n * bps, bps), bps), :]
        return jnp.concatenate([per_head(lambda r: rows), jnp.zeros((NEG_ROWS - bps, tw), F32)], axis=0).astype(BF)

    no_rows = jnp.zeros((NEG_ROWS, tw), BF)
    c0 = jnp.maximum(i - n_back, 0)
    wbase = pl.multiple_of(c0 * tq, tq)
    raw_w = scores(kw_ref[0, pl.ds(wbase, nwin), :], kaug_w_ref[...], no_rows)
    k_aug = kaug_s_ref[...]
    n_last = (t0 + tq - 1) // kcs
    raw_d = scores(ks_ref[0, pl.ds(pl.multiple_of(n_last * kcs, kcs), kcs), :], k_aug, no_rows)

    dist_c = t_row - (_iota((ncmp, 1), 0) * CMP_STRIDE + (CMP_LEN - 1))
    s = (jnp.dot(kc_ref[0], q_all, preferred_element_type=F32) * SCALE2
         + jnp.where(dist_c >= 0, dist_c.astype(F32) * (-sl2_row), NEG))
    e = jnp.exp2(s - jnp.max(s, axis=0, keepdims=True))
    has_block = (t_row >= CMP_LEN - 1).astype(F32)
    p = e * (has_block / jnp.maximum(jnp.sum(e, axis=0, keepdims=True), 1e-30))
    o_cmp = jnp.dot(vct_ref[...], p.astype(BF), preferred_element_type=F32)

    ratio = SLC_LEN // CMP_STRIDE
    assert SLC_LEN % CMP_STRIDE == 0 and CMP_LEN <= SLC_LEN and ncmp >= nsel * ratio
    p_heads = sum(p[:, r * tq:(r + 1) * tq] for r in range(B_REP))
    for lt in range(tq // LANES):
        pc_ref[lt] = p_heads[:, lt * LANES:(lt + 1) * LANES]
    imp = jnp.zeros((nsel, tq), F32)
    for o in range(1 - (-(-CMP_LEN // CMP_STRIDE)), ratio):
        overlap = max(min(o * CMP_STRIDE + CMP_LEN, SLC_LEN) - max(o * CMP_STRIDE, 0), 0)
        if overlap == 0:
            continue
        part = jnp.concatenate([pc_ref[lt, pl.ds(o % ratio, nsel, stride=ratio), :]
                                for lt in range(tq // LANES)], axis=1)
        if o < 0:
            part = jnp.where(_iota((nsel, tq), 0) == 0, 0.0, pltpu.roll(part, 1, 0))
        imp = imp + (overlap / CMP_LEN) * part

    jb = _iota((nsel, tq), 0)
    cur = t_q // SLC_LEN
    forced = (jb == 0) | (jb == cur) | (jb == cur - 1)
    imp = jnp.where(forced, SEL_FORCE, jnp.where(jb > cur, -SEL_FORCE, imp))
    beaten_by = jnp.zeros((nsel, tq), jnp.int32)
    for j in range(nsel):
        row = imp[j:j + 1, :]
        beats = (row > imp) | ((row == imp) & (jb > j))
        beaten_by = beaten_by + beats.astype(jnp.int32)
    neg_ref[...] = jnp.where(beaten_by >= min(SLC_TOPK, nsel), NEG, 0.0)

    def softmax_step(raw, vt_c, mask, c_off, st):
        m, acc = st
        s = raw * SCALE2
        if mask is not None:
            s = s + per_head(lambda r: mask)
        m_new = jnp.maximum(m, jnp.max(s, axis=0, keepdims=True) + c_off)
        p = jnp.exp2(s - (m_new - c_off))
        acc = jnp.exp2(m - m_new) * acc + jnp.dot(vt_c, p.astype(BF), preferred_element_type=F32)
        return m_new, acc

    def finish(st):
        _, acc = st
        return acc[:HEAD_DIM] * (1.0 / acc[HEAD_DIM:HEAD_DIM + 1])

    init = (jnp.full((1, tw), NEG, F32), jnp.zeros((HEAD_DIM + ONES_ROWS, tw), F32))
    no_off = jnp.zeros((1, 1), F32)

    vwt = jnp.concatenate([vwt_ref[c0 + j] for j in range(n_back + 1)], axis=1)
    o_win = finish(softmax_step(raw_w, vwt, wmask_ref[i - c0], no_off, init))

    def slab_scores(n):
        return scores(ks_ref[0, pl.ds(pl.multiple_of(n * kcs, kcs), kcs), :], k_aug, neg_rows_of(n))

    raw_a_ref[...] = slab_scores(0)
    sel_rows = [jnp.broadcast_to(neg_ref[pl.ds(n_last * bps + b, 1), :], (SLC_LEN, tq)) for b in range(bps)]
    mask_d = jnp.concatenate(sel_rows, axis=0) + cmask_ref[i - n_last * (kcs // tq)]
    st = softmax_step(raw_d, vst_ref[n_last], mask_d, (n_last * kcs - t0).astype(F32) * sl2_row, init)

    def sel_slab(n, st, src_ref, dst_ref):
        dst_ref[...] = slab_scores(jnp.minimum(n + 1, n_last))
        return softmax_step(src_ref[...], vst_ref[n], None, (n * kcs - t0).astype(F32) * sl2_row, st)

    def sel_pair(j, st):
        st = sel_slab(2 * j, st, raw_a_ref, raw_b_ref)
        return lax.cond(2 * j + 1 < n_last, lambda s: sel_slab(2 * j + 1, s, raw_b_ref, raw_a_ref),
                        lambda s: s, st)

    o_slc = finish(lax.fori_loop(0, (n_last + 1) // 2, sel_pair, st))

    gate_t = (1.0 / (1.0 + jnp.exp(-gate_ref[0]))).T
    for r in range(B_REP):
        c = slice(r * tq, (r + 1) * tq)
        o_t = (gate_t[3 * r:3 * r + 1] * o_cmp[:, c] + gate_t[3 * r + 1:3 * r + 2] * o_slc[:, c]
               + gate_t[3 * r + 2:3 * r + 3] * o_win[:, c])
        o_ref[0, :, r * HEAD_DIM:(r + 1) * HEAD_DIM] = o_t.T.astype(o_ref.dtype)


def nsa_attention(qp, gates, kc, vc, kv, tq=256):
    B, S, _ = qp.shape
    G = B_KV_GROUPS
    gq = B_REP * HEAD_DIM
    ncmp = kc.shape[1]
    kcs = 2 * tq
    nsel = S // SLC_LEN
    assert kcs % SLC_LEN == 0 and S % kcs == 0 and S >= (-(-(WIN_LEN - 1) // tq) + 1) * tq
    slopes = 2.0 ** (-8.0 * jnp.arange(1, B_HEADS + 1, dtype=F32) / B_HEADS)
    sl = jnp.zeros((G, 8, LANES), F32).at[:, :, :B_REP].set(
        jnp.broadcast_to(slopes.reshape(G, 1, B_REP), (G, 8, B_REP)))
    kvspec = lambda sec: pl.BlockSpec((1, S, HEAD_DIM), lambda b, g, i: (b, 0, sec * G + g))
    cspec = pl.BlockSpec((1, ncmp, HEAD_DIM), lambda b, g, i: (b, 0, g))
    vt_rows = HEAD_DIM + ONES_ROWS
    n_back = -(-(WIN_LEN - 1) // tq)
    nwin = (n_back + 1) * tq
    bps = kcs // SLC_LEN
    assert bps <= NEG_ROWS and NEG_ROWS + 6 <= AUG_K
    consts = [_key_aug(nwin, bps), _key_aug(kcs, bps), _offset_masks(n_back + 1, nwin, tq, 0, WIN_LEN - 1),
              _offset_masks(kcs // tq, kcs, tq, 0, S)]
    whole = lambda a: pl.BlockSpec(a.shape, lambda b, g, i, nd=a.ndim: (0,) * nd)
    return pl.pallas_call(
        functools.partial(_nsa_kernel, tq),
        grid=(B, G, S // tq),
        in_specs=[pl.BlockSpec((1, tq, gq), lambda b, g, i: (b, i, g)),
                  pl.BlockSpec((1, tq, LANES), lambda b, g, i: (b, i, g)),
                  cspec, cspec, kvspec(0), kvspec(1), kvspec(2), kvspec(3),
                  pl.BlockSpec((1, 8, LANES), lambda b, g, i: (g, 0, 0))] + [whole(a) for a in consts],
        out_specs=pl.BlockSpec((1, tq, gq), lambda b, g, i: (b, i, g)),
        out_shape=jax.ShapeDtypeStruct((B, S, B_HEADS * HEAD_DIM), BF),
        scratch_shapes=[pltpu.VMEM((S // kcs, vt_rows, kcs), BF), pltpu.VMEM((S // tq, vt_rows, tq), BF),
                        pltpu.VMEM((HEAD_DIM, ncmp), BF), pltpu.VMEM((nsel, tq), F32),
                        pltpu.VMEM((kcs, B_REP * tq), F32), pltpu.VMEM((kcs, B_REP * tq), F32),
                        pltpu.VMEM((tq // LANES, ncmp, LANES), F32)],
        compiler_params=_params("arbitrary", "arbitrary", "arbitrary"),
        name="nsa_attention",
    )(qp, gates, kc, vc, kv, kv, kv, kv, sl, *consts)


def kernel(x, mem, norm_g, a_w_in, a_w_out, b_w_in, b_w_out, mem_w_kv, ffn_w_gu, ffn_w_down,
           kv_norm_g, kv_w, cmp_pe, cmp_wk1, cmp_wk2, cmp_wv1, cmp_wv2):
    B, S, D = x.shape
    M = B * S
    n_a = DEPTH // 2
    qa_cols = len(DIL_GROUPS) * A_HEADS_PER_GROUP * HEAD_DIM
    qb_cols = B_HEADS * HEAD_DIM
    mq_cols = MEM_HEADS * HEAD_DIM
    n_gate = 3 * B_HEADS
    G = B_KV_GROUPS

    b_w_qm = jnp.concatenate([b_w_in[:, :, :qb_cols], b_w_in[:, :, qb_cols + n_gate:]], axis=-1)
    b_w_gate = jnp.pad(b_w_in[:, :, qb_cols:qb_cols + n_gate].reshape(-1, D, G, n_gate // G),
                       ((0, 0), (0, 0), (0, 0), (0, LANES - n_gate // G))).reshape(-1, D, G * LANES)
    a_w_out_b = a_w_out.astype(BF)
    b_w_out_b = b_w_out.astype(BF)
    w_down_b = ffn_w_down.astype(BF)
    kv_w3 = kv_w[None]

    x2 = x.reshape(M, D)
    mem2 = mem.reshape(B * N_MEM, D)
    u = rmsnorm_rows(x2, norm_g[0, 0])
    shared = None
    for l in range(DEPTH):
        g = norm_g[l]
        mem_n = rmsnorm_rows(mem2, g[4])
        mkv = matmul(mem_n, mem_w_kv, l, name="mem_kv").reshape(B, N_MEM, 2 * mq_cols)
        if l < n_a:
            q = matmul(u, a_w_in, l, col0=0, ncols=qa_cols, out_dtype=F32, name="a_q").reshape(B, S, qa_cols)
            k = matmul(u, a_w_in, l, col0=qa_cols, ncols=qa_cols, out_dtype=F32, name="a_k").reshape(B, S, qa_cols)
            v = matmul(u, a_w_in, l, col0=2 * qa_cols, ncols=qa_cols, out_dtype=F32,
                       name="a_v").reshape(B, S, qa_cols)
            mq = matmul(u, a_w_in, l, col0=3 * qa_cols, ncols=mq_cols, name="a_mq").reshape(B, S, mq_cols)
            o_main = dilated_attention(q, k, v)
            o_mem = memory_attention(mq, 0, mkv)
            w_out = a_w_out_b
            li = l
        else:
            li = l - n_a
            if shared is None:
                kvn = rmsnorm_rows(x2, kv_norm_g)
                kvc = matmul(kvn, kv_w3, 0, col0=0, ncols=2 * G * HEAD_DIM, out_dtype=F32,
                             name="kv_cmp").reshape(B, S, 2 * G * HEAD_DIM)
                kvr = matmul(kvn, kv_w3, 0, col0=2 * G * HEAD_DIM, ncols=4 * G * HEAD_DIM,
                             name="kv_rest").reshape(B, S, 4 * G * HEAD_DIM)
                kc = compress_blocks(kvc, 0, cmp_wk1, cmp_wk2, cmp_pe[0])
                vc = compress_blocks(kvc, 1, cmp_wv1, cmp_wv2, cmp_pe[1])
                shared = (kc, vc, kvr)
            qm = matmul(u, b_w_qm, li, name="b_qm").reshape(B, S, qb_cols + mq_cols)
            gates = matmul(u, b_w_gate, li, out_dtype=F32, name="b_gate").reshape(B, S, G * LANES)
            o_main = nsa_attention(qm, gates, *shared)
            o_mem = memory_attention(qm, qb_cols, mkv)
            w_out = b_w_out_b
        x2, u = proj_residual_norm(o_main.reshape(M, -1), o_mem.reshape(M, mq_cols), 0, mq_cols,
                                   w_out, li, x2, g[1], g[2])
        h = ffn_gate_up(u, ffn_w_gu, l)
        gn = norm_g[l + 1, 0] if l + 1 < DEPTH else None
        x2, u = proj_residual_norm(h, None, 0, 0, w_down_b, l, x2, g[3], gn, tm=256, n_sub=1)
    return x2.reshape(B, S, D)
```

```python
import functools
import math

import jax
import jax.numpy as jnp
from jax import lax
from jax.experimental import pallas as pl
from jax.experimental.pallas import tpu as pltpu

BF = jnp.bfloat16
F32 = jnp.float32

D_MODEL = 2048
DEPTH = 4
HEAD_DIM = 128
N_MEM = 256
MEM_HEADS = 4
DIL_GROUPS = ((128, 1), (512, 4), (2048, 16))
A_HEADS_PER_GROUP = 8
A_BLOCK = 128
B_HEADS = 12
B_KV_GROUPS = 4
B_REP = B_HEADS // B_KV_GROUPS
CMP_LEN = 32
CMP_STRIDE = 16
CMP_HIDDEN = 512
SLC_LEN = 64
SLC_TOPK = 16
SEL_FORCE = 1e4
WIN_LEN = 512
D_FF = 5632
EPS = 1e-6

LANES = 128
VMEM_LIMIT = 56 * 1024 * 1024
NEG = -1e30
SCALE = 1.0 / math.sqrt(HEAD_DIM)
LOG2E = math.log2(math.e)
LN2 = math.log(2.0)
SCALE2 = SCALE * LOG2E
NT = (((1,), (1,)), ((), ()))
DIL_GROUP_BLOCKS = 8
STAGE_STRIDE = 4


def _params(*sem):
    return pltpu.CompilerParams(dimension_semantics=sem, vmem_limit_bytes=VMEM_LIMIT)


def _masked_softmax(s, valid):
    s = jnp.where(valid, s, NEG)
    m = jnp.max(s, axis=-1, keepdims=True)
    e = jnp.where(valid, jnp.exp(s - m), 0.0)
    den = jnp.maximum(jnp.sum(e, axis=-1, keepdims=True), 1e-30)
    return e * (1.0 / den)


def _rms(x, g):
    return x * lax.rsqrt(jnp.mean(x * x, axis=-1, keepdims=True) + EPS) * g


def _iota(shape, dim):
    return lax.broadcasted_iota(jnp.int32, shape, dim)


def _rmsnorm_kernel(x_ref, g_ref, o_ref):
    o_ref[...] = _rms(x_ref[...], g_ref[...]).astype(o_ref.dtype)


def rmsnorm_rows(x, g, tm=512):
    M, D = x.shape
    tm = min(tm, M)
    return pl.pallas_call(
        _rmsnorm_kernel,
        grid=(M // tm,),
        in_specs=[pl.BlockSpec((tm, D), lambda i: (i, 0)),
                  pl.BlockSpec((1, D), lambda i: (0, 0))],
        out_specs=pl.BlockSpec((tm, D), lambda i: (i, 0)),
        out_shape=jax.ShapeDtypeStruct((M, D), BF),
        compiler_params=_params("arbitrary"),
        name="rmsnorm",
    )(x, g.reshape(1, D))


def _mm_kernel(x_ref, w_ref, o_ref, wbf_ref):
    j = pl.program_id(1)

    @pl.when(pl.program_id(0) == 0)
    def _():
        wbf_ref[j] = w_ref[...].astype(BF)

    o_ref[...] = jnp.dot(x_ref[...], wbf_ref[j],
                         preferred_element_type=F32).astype(o_ref.dtype)


def matmul(x, w, layer, *, col0=0, ncols=None, tm=1024, out_dtype=BF, name="matmul"):
    M, K = x.shape
    ncols = w.shape[2] - col0 if ncols is None else ncols
    tn = 1024 if (ncols % 1024 == 0 and col0 % 1024 == 0) else 512
    tm = min(tm, M)
    assert col0 % tn == 0 and ncols % tn == 0 and M % tm == 0
    c0 = col0 // tn
    return pl.pallas_call(
        _mm_kernel,
        grid=(M // tm, ncols // tn),
        in_specs=[pl.BlockSpec((tm, K), lambda i, j: (i, 0)),
                  pl.BlockSpec((None, K, tn), lambda i, j: (layer, 0, jnp.where(i == 0, j, 0) + c0))],
        out_specs=pl.BlockSpec((tm, tn), lambda i, j: (i, j)),
        out_shape=jax.ShapeDtypeStruct((M, ncols), out_dtype),
        scratch_shapes=[pltpu.VMEM((ncols // tn, K, tn), BF)],
        compiler_params=_params("arbitrary", "arbitrary"),
        name=name,
    )(x, w)


def _gu_kernel(x_ref, wg_ref, wu_ref, o_ref, wgb_ref, wub_ref):
    @pl.when(pl.program_id(1) == 0)
    def _():
        wgb_ref[...] = wg_ref[...].astype(BF)
        wub_ref[...] = wu_ref[...].astype(BF)

    x = x_ref[...]
    g = jnp.dot(x, wgb_ref[...], preferred_element_type=F32)
    u = jnp.dot(x, wub_ref[...], preferred_element_type=F32)
    o_ref[...] = (g * (1.0 / (1.0 + jnp.exp(-g))) * u).astype(o_ref.dtype)


def ffn_gate_up(x, w, layer, tn=512, tm=1024):
    M, K = x.shape
    nj = D_FF // tn
    return pl.pallas_call(
        _gu_kernel,
        grid=(nj, M // tm),
        in_specs=[pl.BlockSpec((tm, K), lambda j, i: (i, 0)),
                  pl.BlockSpec((None, K, tn), lambda j, i: (layer, 0, j)),
                  pl.BlockSpec((None, K, tn), lambda j, i: (layer, 0, j + nj))],
        out_specs=pl.BlockSpec((tm, tn), lambda j, i: (i, j)),
        out_shape=jax.ShapeDtypeStruct((M, D_FF), BF),
        scratch_shapes=[pltpu.VMEM((K, tn), BF), pltpu.VMEM((K, tn), BF)],
        compiler_params=_params("arbitrary", "arbitrary"),
        name="ffn_gate_up",
    )(x, w, w)


def _proj_res_kernel(has_x2, has_next, n_sub, *refs):
    refs = list(refs)
    x1_ref = refs.pop(0)
    x2_ref = refs.pop(0) if has_x2 else None
    w_ref, xres_ref, g_ref = refs.pop(0), refs.pop(0), refs.pop(0)
    gn_ref = refs.pop(0) if has_next else None
    xo_ref = refs.pop(0)
    uo_ref = refs.pop(0) if has_next else None
    k1 = x1_ref.shape[1]
    sub = xo_ref.shape[0] // n_sub
    for h in range(n_sub):
        rows = slice(h * sub, (h + 1) * sub)
        acc = jnp.dot(x1_ref[rows, :], w_ref[:k1, :], preferred_element_type=F32)
        if has_x2:
            acc = acc + jnp.dot(x2_ref[rows, :], w_ref[k1:, :], preferred_element_type=F32)
        xn = xres_ref[rows, :] + _rms(acc, g_ref[...])
        xo_ref[rows, :] = xn
        if has_next:
            uo_ref[rows, :] = _rms(xn, gn_ref[...]).astype(uo_ref.dtype)


def proj_residual_norm(x1, x2, x2_col0, k2, w, layer, xres, g, gn, tm=512, n_sub=2):
    M, K1 = x1.shape
    K, N = w.shape[1], w.shape[2]
    has_x2, has_next = x2 is not None, gn is not None
    assert K == K1 + (k2 if has_x2 else 0)
    row = lambda i: (i, 0)
    in_specs = [pl.BlockSpec((tm, K1), row)]
    args = [x1]
    if has_x2:
        assert x2_col0 % k2 == 0
        in_specs.append(pl.BlockSpec((tm, k2), lambda i: (i, x2_col0 // k2)))
        args.append(x2)
    in_specs += [pl.BlockSpec((None, K, N), lambda i: (layer, 0, 0), pipeline_mode=pl.Buffered(1)),
                 pl.BlockSpec((tm, N), row),
                 pl.BlockSpec((1, N), lambda i: (0, 0))]
    args += [w, xres, g.reshape(1, N)]
    out_specs = [pl.BlockSpec((tm, N), row)]
    out_shape = [jax.ShapeDtypeStruct((M, N), F32)]
    if has_next:
        in_specs.append(pl.BlockSpec((1, N), lambda i: (0, 0)))
        args.append(gn.reshape(1, N))
        out_specs.append(pl.BlockSpec((tm, N), row))
        out_shape.append(jax.ShapeDtypeStruct((M, N), BF))
    res = pl.pallas_call(
        functools.partial(_proj_res_kernel, has_x2, has_next, n_sub),
        grid=(M // tm,),
        in_specs=in_specs,
        out_specs=out_specs,
        out_shape=out_shape,
        compiler_params=_params("arbitrary"),
        name="proj_residual_norm",
    )(*args)
    return (res[0], res[1]) if has_next else (res[0], None)


def _divmod_pow2(n, d):
    assert d & (d - 1) == 0
    return lax.shift_right_logical(n, d.bit_length() - 1), n & (d - 1)


def _dil_kernel(*refs):
    ng = len(DIL_GROUPS)
    q_refs, k_refs, v_refs = refs[:ng], refs[ng:2 * ng], refs[2 * ng:3 * ng]
    o_ref, o_scr, l_scr, stage_scr, ostage_scr = refs[3 * ng:]
    S = o_ref.shape[1]
    blk = A_BLOCK
    n_heads = ng * A_HEADS_PER_GROUP
    j = pl.program_id(1)
    for gi, (w, d) in enumerate(DIL_GROUPS):
        q_ref, k_ref, v_ref = q_refs[gi], k_refs[gi], v_refs[gi]
        nqb = S // d // blk
        max_dist = w // d
        has_prev = nqb > 1
        nk = 2 * blk if has_prev else blk
        koff = blk if has_prev else 0
        head = (j + (gi * A_HEADS_PER_GROUP + 1)).astype(F32)
        slope = jnp.exp2(jnp.zeros((1, 1), F32) + head * (-8.0 / n_heads))
        kidx = _iota((blk, nk), 1)
        dist = _iota((blk, nk), 0) + koff - kidx
        band = (dist >= 0) & (dist <= max_dist)
        bias = jnp.where(band, (kidx - koff).astype(F32) * (slope * (d * LOG2E)), NEG)
        bias_first = jnp.where(kidx >= koff, bias, NEG)
        row_term = _iota((blk, 1), 0).astype(F32) * (slope * (-float(d)))

        staged = d > STAGE_STRIDE
        if staged:
            assert d % STAGE_STRIDE == 0 and d // STAGE_STRIDE <= STAGE_STRIDE
            quarter = S // STAGE_STRIDE
            for t_idx, ref in enumerate((q_ref, k_ref, v_ref)):
                for r1 in range(STAGE_STRIDE):
                    stage_scr[t_idx, r1 * quarter:(r1 + 1) * quarter, :] = ref[0, pl.ds(r1, quarter, stride=STAGE_STRIDE), :]

        def out_rows(r, i, d=d):
            start = r + i * (d * blk)
            return pl.ds(start, blk, stride=d) if d > 1 else pl.ds(start, blk)

        def staged_rows(r, i, d=d):
            s2 = d // STAGE_STRIDE
            r2, r1 = _divmod_pow2(r, STAGE_STRIDE)
            return pl.ds(r1 * (S // STAGE_STRIDE) + r2 + i * (s2 * blk), blk, stride=s2)

        def load(t_idx, ref, r, i, staged=staged):
            return stage_scr[t_idx, staged_rows(r, i), :] if staged else ref[0, out_rows(r, i), :]

        def body(n0, carry, gi=gi, d=d, has_prev=has_prev, q_ref=q_ref, k_ref=k_ref, v_ref=v_ref,
                 bias=bias, bias_first=bias_first, row_term=row_term, out_rows=out_rows, load=load,
                 staged=staged, staged_rows=staged_rows):
            ri = [_divmod_pow2(n0 * DIL_GROUP_BLOCKS + g, d)[::-1] for g in range(DIL_GROUP_BLOCKS)]
            qs = [load(0, q_ref, r, i).astype(BF) for r, i in ri]
            if has_prev:
                ips = [jnp.maximum(i - 1, 0) for _, i in ri]
                ks = [jnp.concatenate([load(1, k_ref, r, ip), load(1, k_ref, r, i)], axis=0).astype(BF)
                      for (r, i), ip in zip(ri, ips)]
                vs = [jnp.concatenate([load(2, v_ref, r, ip), load(2, v_ref, r, i)], axis=0).astype(BF)
                      for (r, i), ip in zip(ri, ips)]
                bs = [jnp.where(i > 0, bias, bias_first) for _, i in ri]
            else:
                ks = [load(1, k_ref, r, i).astype(BF) for r, i in ri]
                vs = [load(2, v_ref, r, i).astype(BF) for r, i in ri]
                bs = [bias] * len(ri)
            ss = [lax.dot_general(q, k, NT, preferred_element_type=F32) * SCALE2 + b for q, k, b in zip(qs, ks, bs)]
            ms = [jnp.max(s, axis=-1, keepdims=True) for s in ss]
            ps = [jnp.exp2(s - m) for s, m in zip(ss, ms)]
            ls = [jnp.sum(p, axis=-1, keepdims=True) for p in ps]
            os_ = [jnp.dot(p.astype(BF), v, preferred_element_type=F32) * (1.0 / l) for p, v, l in zip(ps, vs, ls)]
            for (r, i), o, m, l in zip(ri, os_, ms, ls):
                lse_b = jnp.broadcast_to((m + jnp.log2(l)) * LN2 + row_term, (blk, HEAD_DIM))
                if staged:
                    rows = staged_rows(r, i)
                    ostage_scr[0, rows, :] = o
                    ostage_scr[1, rows, :] = lse_b
                else:
                    rows = out_rows(r, i)
                    o_scr[gi, rows, :] = o
                    l_scr[gi, rows, :] = lse_b
            return carry

        assert (d * nqb) % DIL_GROUP_BLOCKS == 0
        lax.fori_loop(0, d * nqb // DIL_GROUP_BLOCKS, body, 0)
        if staged:
            for r1 in range(STAGE_STRIDE):
                dst = pl.ds(r1, quarter, stride=STAGE_STRIDE)
                o_scr[gi, dst, :] = ostage_scr[0, r1 * quarter:(r1 + 1) * quarter, :]
                l_scr[gi, dst, :] = ostage_scr[1, r1 * quarter:(r1 + 1) * quarter, :]

    tr = 256
    for c in range(S // tr):
        rows = slice(c * tr, (c + 1) * tr)
        ls = [l_scr[gi, rows, :] for gi in range(ng)]
        m = functools.reduce(jnp.maximum, ls)
        es = [jnp.exp(x - m) for x in ls]
        num = sum(e * o_scr[gi, rows, :] for gi, e in enumerate(es))
        o_ref[0, rows, :] = (num * (1.0 / sum(es))).astype(o_ref.dtype)


def dilated_attention(q, k, v):
    B, S, _ = q.shape
    ng, Hg = len(DIL_GROUPS), A_HEADS_PER_GROUP
    for w, d in DIL_GROUPS:
        assert w // d <= A_BLOCK and S % (d * A_BLOCK) == 0
    specs = [pl.BlockSpec((1, S, HEAD_DIM), lambda b, j, gi=gi: (b, 0, gi * Hg + j)) for gi in range(ng)]
    return pl.pallas_call(
        _dil_kernel,
        grid=(B, Hg),
        in_specs=specs * 3,
        out_specs=pl.BlockSpec((1, S, HEAD_DIM), lambda b, j: (b, 0, j)),
        out_shape=jax.ShapeDtypeStruct((B, S, Hg * HEAD_DIM), BF),
        scratch_shapes=[pltpu.VMEM((ng, S, HEAD_DIM), F32), pltpu.VMEM((ng, S, HEAD_DIM), F32),
                        pltpu.VMEM((3, S, HEAD_DIM), F32), pltpu.VMEM((2, S, HEAD_DIM), F32)],
        compiler_params=_params("arbitrary", "arbitrary"),
        name="dilated_attention",
    )(*([q] * ng + [k] * ng + [v] * ng))


def _mem_attn_kernel(q_ref, kv_ref, o_ref):
    for h in range(MEM_HEADS):
        c = slice(h * HEAD_DIM, (h + 1) * HEAD_DIM)
        k = kv_ref[0, :, c]
        v = kv_ref[0, :, MEM_HEADS * HEAD_DIM + h * HEAD_DIM:MEM_HEADS * HEAD_DIM + (h + 1) * HEAD_DIM]
        s = lax.dot_general(q_ref[0, :, c], k, NT, preferred_element_type=F32) * SCALE
        m = jnp.max(s, axis=-1, keepdims=True)
        e = jnp.exp(s - m)
        p = e * (1.0 / jnp.sum(e, axis=-1, keepdims=True))
        o_ref[0, :, c] = jnp.dot(p.astype(BF), v, preferred_element_type=F32).astype(o_ref.dtype)


def memory_attention(q, q_col0, kv, tq=512):
    B, S, _ = q.shape
    w = MEM_HEADS * HEAD_DIM
    cb = q_col0 // w
    return pl.pallas_call(
        _mem_attn_kernel,
        grid=(B, S // tq),
        in_specs=[pl.BlockSpec((1, tq, w), lambda b, i: (b, i, cb)),
                  pl.BlockSpec((1, N_MEM, 2 * w), lambda b, i: (b, 0, 0))],
        out_specs=pl.BlockSpec((1, tq, w), lambda b, i: (b, i, 0)),
        out_shape=jax.ShapeDtypeStruct((B, S, w), BF),
        compiler_params=_params("arbitrary", "arbitrary"),
        name="memory_attention",
    )(q, kv)


def _gelu_tanh(x):
    return 0.5 * x * (1.0 + jnp.tanh(math.sqrt(2.0 / math.pi) * (x + 0.044715 * (x * x * x))))


def _compress_kernel(x_ref, w1_ref, w2_ref, pe_ref, o_ref, w1b_ref):
    @pl.when((pl.program_id(0) == 0) & (pl.program_id(1) == 0))
    def _():
        w1b_ref[...] = w1_ref[...].astype(BF)

    nchunk = o_ref.shape[1]
    top = jnp.zeros((nchunk, CMP_HIDDEN), F32)
    bot = jnp.zeros((nchunk, CMP_HIDDEN), F32)
    for l in range(CMP_STRIDE):
        xl = x_ref[0, pl.ds(l, nchunk, stride=CMP_STRIDE), :]
        lo, hi = l, CMP_STRIDE + l
        top += jnp.dot((xl + pe_ref[lo:lo + 1, :]).astype(BF),
                       w1b_ref[lo * HEAD_DIM:(lo + 1) * HEAD_DIM, :], preferred_element_type=F32)
        bot += jnp.dot((xl + pe_ref[hi:hi + 1, :]).astype(BF),
                       w1b_ref[hi * HEAD_DIM:(hi + 1) * HEAD_DIM, :], preferred_element_type=F32)
    hid = _gelu_tanh(top + pltpu.roll(bot, nchunk - 1, 0))
    o_ref[0] = jnp.dot(hid.astype(BF), w2_ref[...].astype(BF),
                       preferred_element_type=F32).astype(o_ref.dtype)


def compress_blocks(kvc, sec, w1, w2, pe):
    B, S, C = kvc.shape
    assert CMP_LEN == 2 * CMP_STRIDE
    G = B_KV_GROUPS
    nchunk = S // CMP_STRIDE
    return pl.pallas_call(
        _compress_kernel,
        grid=(B, G),
        in_specs=[pl.BlockSpec((1, S, HEAD_DIM), lambda b, g: (b, 0, sec * G + g)),
                  pl.BlockSpec(w1.shape, lambda b, g: (0, 0)),
                  pl.BlockSpec(w2.shape, lambda b, g: (0, 0)),
                  pl.BlockSpec(pe.shape, lambda b, g: (0, 0))],
        out_specs=pl.BlockSpec((1, nchunk, HEAD_DIM), lambda b, g: (b, 0, g)),
        out_shape=jax.ShapeDtypeStruct((B, nchunk, G * HEAD_DIM), BF),
        scratch_shapes=[pltpu.VMEM(w1.shape, BF)],
        compiler_params=_params("arbitrary", "arbitrary"),
        name=f"compress_{sec}",
    )(kvc, w1, w2, pe)


ONES_ROWS = 16
AUG_K = LANES
NEG_ROWS = 16


def _key_aug(nk, bps):
    idx = _iota((nk, AUG_K), 0)
    lane = _iota((nk, AUG_K), 1)
    piece = jnp.where((lane - NEG_ROWS) % 2 == 0, (idx // 256) * 256, idx % 256)
    val = jnp.where(lane < bps, (idx // SLC_LEN == lane).astype(jnp.int32),
                    jnp.where((lane >= NEG_ROWS) & (lane < NEG_ROWS + 6), piece, 0))
    return val.astype(F32).astype(BF)


def _offset_masks(n, nk, tq, lo, hi):
    dist = (_iota((n, nk, tq), 0) * tq + _iota((n, nk, tq), 2)) - _iota((n, nk, tq), 1)
    return jnp.where((dist >= lo) & (dist <= hi), 0.0, NEG).astype(F32)


def _t_bf(x):
    return x.astype(F32).T.astype(BF)


def _nsa_kernel(tq, q_ref, gate_ref, kc_ref, vc_ref, ks_ref, vs_ref, kw_ref, vw_ref,
                sl_ref, kaug_w_ref, kaug_s_ref, wmask_ref, cmask_ref, o_ref,
                vst_ref, vwt_ref, qt_ref, ocmp_ref, neg_ref, raw_a_ref, raw_b_ref, pc_ref):
    S = ks_ref.shape[1]
    ncmp = kc_ref.shape[1]
    nsel = S // SLC_LEN
    i = pl.program_id(2)

    t0 = i * tq
    tw = B_REP * tq
    nt = S // tq
    n_back = -(-(WIN_LEN - 1) // tq)
    nwin = (n_back + 1) * tq
    kcs = vst_ref.shape[2]
    bps = kcs // SLC_LEN

    def per_head(fn, n=B_REP):
        return jnp.concatenate([fn(r) for r in range(n)], axis=1)

    def bf_exact(x):
        return x.astype(BF).astype(F32)

    @pl.when(i == 0)
    def _():
        for src_ref, dst_ref in ((vs_ref, vst_ref), (vw_ref, vwt_ref)):
            n, _, w = dst_ref.shape
            ones = jnp.ones((ONES_ROWS, w), BF)
            for c in range(n):
                dst_ref[c] = jnp.concatenate([_t_bf(src_ref[0, c * w:(c + 1) * w, :]), ones], axis=0)
        vct = _t_bf(vc_ref[0])
        for t in range(nt):
            for r in range(B_REP):
                qt_ref[t, r] = _t_bf(q_ref[0, t * tq:(t + 1) * tq, r * HEAD_DIM:(r + 1) * HEAD_DIM])
        q_w = jnp.concatenate([qt_ref[t, r] for r in range(B_REP) for t in range(nt)], axis=1)
        sl2_w = per_head(lambda r: jnp.broadcast_to(sl_ref[0, 0:1, r:r + 1], (1, S))) * LOG2E
        pos = _iota((1, S), 1)
        pos_w = per_head(lambda r: pos)

        dist_c = pos_w - (_iota((ncmp, 1), 0) * CMP_STRIDE + (CMP_LEN - 1))
        s = (jnp.dot(kc_ref[0], q_w, preferred_element_type=F32) * SCALE2
             + jnp.where(dist_c >= 0, dist_c.astype(F32) * (-sl2_w), NEG))
        e = jnp.exp2(s - jnp.max(s, axis=0, keepdims=True))
        has_block = (pos_w >= CMP_LEN - 1).astype(F32)
        p = e * (has_block / jnp.maximum(jnp.sum(e, axis=0, keepdims=True), 1e-30))
        o_cmp_w = jnp.dot(vct, p.astype(BF), preferred_element_type=F32)
        for t in range(nt):
            for r in range(B_REP):
                ocmp_ref[t, r] = o_cmp_w[:, r * S + t * tq:r * S + (t + 1) * tq]

        ratio = SLC_LEN // CMP_STRIDE
        assert SLC_LEN % CMP_STRIDE == 0 and CMP_LEN <= SLC_LEN and ncmp >= nsel * ratio
        p_heads = sum(p[:, r * S:(r + 1) * S] for r in range(B_REP))
        for lt in range(S // LANES):
            pc_ref[lt] = p_heads[:, lt * LANES:(lt + 1) * LANES]
        imp = jnp.zeros((nsel, S), F32)
        for o in range(1 - (-(-CMP_LEN // CMP_STRIDE)), ratio):
            overlap = max(min(o * CMP_STRIDE + CMP_LEN, SLC_LEN) - max(o * CMP_STRIDE, 0), 0)
            if overlap == 0:
                continue
            part = jnp.concatenate([pc_ref[lt, pl.ds(o % ratio, nsel, stride=ratio), :]
                                    for lt in range(S // LANES)], axis=1)
            if o < 0:
                part = jnp.where(_iota((nsel, S), 0) == 0, 0.0, pltpu.roll(part, 1, 0))
            imp = imp + (overlap / CMP_LEN) * part

        jb = _iota((nsel, S), 0)
        cur = pos // SLC_LEN
        forced = (jb == 0) | (jb == cur) | (jb == cur - 1)
        imp = jnp.where(forced, SEL_FORCE, jnp.where(jb > cur, -SEL_FORCE, imp))
        beaten_by = jnp.zeros((nsel, S), jnp.int32)
        for j in range(nsel):
            row = imp[j:j + 1, :]
            beats = (row > imp) | ((row == imp) & (jb > j))
            beaten_by = beaten_by + beats.astype(jnp.int32)
        neg_w = jnp.where(beaten_by >= min(SLC_TOPK, nsel), NEG, 0.0)
        for t in range(nt):
            neg_ref[t] = neg_w[:, t * tq:(t + 1) * tq]

    q_all = per_head(lambda r: qt_ref[i, r])
    o_cmp = per_head(lambda r: ocmp_ref[i, r])
    sl_row = per_head(lambda r: jnp.broadcast_to(sl_ref[0, 0:1, r:r + 1], (1, tq)))
    sl2_row = sl_row * LOG2E

    c1 = bf_exact(sl_row * (1.0 / SCALE))
    c2 = bf_exact(sl_row * (1.0 / SCALE) - c1)
    c3 = bf_exact(sl_row * (1.0 / SCALE) - c1 - c2)
    slope_rows = jnp.concatenate([c1, c1, c2, c2, c3, c3, jnp.zeros((AUG_K - NEG_ROWS - 6, tw), F32)],
                                 axis=0).astype(BF)

    def scores(k_c, k_aug, neg_rows):
        q_full = jnp.concatenate([q_all, neg_rows, slope_rows], axis=0)
        return jnp.dot(jnp.concatenate([k_c, k_aug], axis=1), q_full, preferred_element_type=F32)

    def neg_rows_of(n):
        rows = neg_ref[i, pl.ds(pl.multiple_of(n * bps, bps), bps), :]
        return jnp.concatenate([per_head(lambda r: rows), jnp.zeros((NEG_ROWS - bps, tw), F32)], axis=0).astype(BF)

    no_rows = jnp.zeros((NEG_ROWS, tw), BF)
    c0 = jnp.maximum(i - n_back, 0)
    wbase = pl.multiple_of(c0 * tq, tq)
    raw_w = scores(kw_ref[0, pl.ds(wbase, nwin), :], kaug_w_ref[...], no_rows)
    k_aug = kaug_s_ref[...]
    n_last = (t0 + tq - 1) // kcs
    raw_d = scores(ks_ref[0, pl.ds(pl.multiple_of(n_last * kcs, kcs), kcs), :], k_aug, no_rows)

    def softmax_step(raw, vt_c, mask, c_off, st):
        m, acc = st
        s = raw * SCALE2
        if mask is not None:
            s = s + per_head(lambda r: mask)
        m_new = jnp.maximum(m, jnp.max(s, axis=0, keepdims=True) + c_off)
        p = jnp.exp2(s - (m_new - c_off))
        acc = jnp.exp2(m - m_new) * acc + jnp.dot(vt_c, p.astype(BF), preferred_element_type=F32)
        return m_new, acc

    def finish(st):
        _, acc = st
        return acc[:HEAD_DIM] * (1.0 / acc[HEAD_DIM:HEAD_DIM + 1])

    init = (jnp.full((1, tw), NEG, F32), jnp.zeros((HEAD_DIM + ONES_ROWS, tw), F32))
    no_off = jnp.zeros((1, 1), F32)

    vwt = jnp.concatenate([vwt_ref[c0 + j] for j in range(n_back + 1)], axis=1)
    o_win = finish(softmax_step(raw_w, vwt, wmask_ref[i - c0], no_off, init))

    def slab_scores(n):
        return scores(ks_ref[0, pl.ds(pl.multiple_of(n * kcs, kcs), kcs), :], k_aug, neg_rows_of(n))

    raw_a_ref[...] = slab_scores(0)
    sel_rows = [jnp.broadcast_to(neg_ref[i, pl.ds(n_last * bps + b, 1), :], (SLC_LEN, tq)) for b in range(bps)]
    mask_d = jnp.concatenate(sel_rows, axis=0) + cmask_ref[i - n_last * (kcs // tq)]
    st = softmax_step(raw_d, vst_ref[n_last], mask_d, (n_last * kcs - t0).astype(F32) * sl2_row, init)

    def sel_slab(n, st, src_ref, dst_ref):
        dst_ref[...] = slab_scores(jnp.minimum(n + 1, n_last))
        return softmax_step(src_ref[...], vst_ref[n], None, (n * kcs - t0).astype(F32) * sl2_row, st)

    def sel_pair(j, st):
        st = sel_slab(2 * j, st, raw_a_ref, raw_b_ref)
        return lax.cond(2 * j + 1 < n_last, lambda s: sel_slab(2 * j + 1, s, raw_b_ref, raw_a_ref),
                        lambda s: s, st)

    o_slc = finish(lax.fori_loop(0, (n_last + 1) // 2, sel_pair, st))

    gate_t = (1.0 / (1.0 + jnp.exp(-gate_ref[0]))).T
    for r in range(B_REP):
        c = slice(r * tq, (r + 1) * tq)
        o_t = (gate_t[3 * r:3 * r + 1] * o_cmp[:, c] + gate_t[3 * r + 1:3 * r + 2] * o_slc[:, c]
               + gate_t[3 * r + 2:3 * r + 3] * o_win[:, c])
        o_ref[0, :, r * HEAD_DIM:(r + 1) * HEAD_DIM] = o_t.T.astype(o_ref.dtype)


def nsa_attention(qp, gates, kc, vc, kv, tq=256):
    B, S, _ = qp.shape
    G = B_KV_GROUPS
    gq = B_REP * HEAD_DIM
    ncmp = kc.shape[1]
    kcs = 2 * tq
    nsel = S // SLC_LEN
    assert kcs % SLC_LEN == 0 and S % kcs == 0 and S >= (-(-(WIN_LEN - 1) // tq) + 1) * tq
    slopes = 2.0 ** (-8.0 * jnp.arange(1, B_HEADS + 1, dtype=F32) / B_HEADS)
    sl = jnp.zeros((G, 8, LANES), F32).at[:, :, :B_REP].set(
        jnp.broadcast_to(slopes.reshape(G, 1, B_REP), (G, 8, B_REP)))
    kvspec = lambda sec: pl.BlockSpec((1, S, HEAD_DIM), lambda b, g, i: (b, 0, sec * G + g))
    cspec = pl.BlockSpec((1, ncmp, HEAD_DIM), lambda b, g, i: (b, 0, g))
    vt_rows = HEAD_DIM + ONES_ROWS
    n_back = -(-(WIN_LEN - 1) // tq)
    nwin = (n_back + 1) * tq
    bps = kcs // SLC_LEN
    assert bps <= NEG_ROWS and NEG_ROWS + 6 <= AUG_K
    consts = [_key_aug(nwin, bps), _key_aug(kcs, bps), _offset_masks(n_back + 1, nwin, tq, 0, WIN_LEN - 1),
              _offset_masks(kcs // tq, kcs, tq, 0, S)]
    whole = lambda a: pl.BlockSpec(a.shape, lambda b, g, i, nd=a.ndim: (0,) * nd)
    return pl.pallas_call(
        functools.partial(_nsa_kernel, tq),
        grid=(B, G, S // tq),
        in_specs=[pl.BlockSpec((1, S, gq), lambda b, g, i: (b, 0, g)),
                  pl.BlockSpec((1, tq, LANES), lambda b, g, i: (b, i, g)),
                  cspec, cspec, kvspec(0), kvspec(1), kvspec(2), kvspec(3),
                  pl.BlockSpec((1, 8, LANES), lambda b, g, i: (g, 0, 0))] + [whole(a) for a in consts],
        out_specs=pl.BlockSpec((1, tq, gq), lambda b, g, i: (b, i, g)),
        out_shape=jax.ShapeDtypeStruct((B, S, B_HEADS * HEAD_DIM), BF),
        scratch_shapes=[pltpu.VMEM((S // kcs, vt_rows, kcs), BF), pltpu.VMEM((S // tq, vt_rows, tq), BF),
                        pltpu.VMEM((S // tq, B_REP, HEAD_DIM, tq), BF), pltpu.VMEM((S // tq, B_REP, HEAD_DIM, tq), F32),
                        pltpu.VMEM((S // tq, nsel, tq), F32),
                        pltpu.VMEM((kcs, B_REP * tq), F32), pltpu.VMEM((kcs, B_REP * tq), F32),
                        pltpu.VMEM((S // LANES, ncmp, LANES), F32)],
        compiler_params=_params("arbitrary", "arbitrary", "arbitrary"),
        name="nsa_attention",
    )(qp, gates, kc, vc, kv, kv, kv, kv, sl, *consts)


def kernel(x, mem, norm_g, a_w_in, a_w_out, b_w_in, b_w_out, mem_w_kv, ffn_w_gu, ffn_w_down,
           kv_norm_g, kv_w, cmp_pe, cmp_wk1, cmp_wk2, cmp_wv1, cmp_wv2):
    B, S, D = x.shape
    M = B * S
    n_a = DEPTH // 2
    qa_cols = len(DIL_GROUPS) * A_HEADS_PER_GROUP * HEAD_DIM
    qb_cols = B_HEADS * HEAD_DIM
    mq_cols = MEM_HEADS * HEAD_DIM
    n_gate = 3 * B_HEADS
    G = B_KV_GROUPS

    b_w_qm = jnp.concatenate([b_w_in[:, :, :qb_cols], b_w_in[:, :, qb_cols + n_gate:]], axis=-1)
    b_w_gate = jnp.pad(b_w_in[:, :, qb_cols:qb_cols + n_gate].reshape(-1, D, G, n_gate // G),
                       ((0, 0), (0, 0), (0, 0), (0, LANES - n_gate // G))).reshape(-1, D, G * LANES)
    a_w_out_b = a_w_out.astype(BF)
    b_w_out_b = b_w_out.astype(BF)
    w_down_b = ffn_w_down.astype(BF)
    kv_w3 = kv_w[None]

    x2 = x.reshape(M, D)
    mem2 = mem.reshape(B * N_MEM, D)
    u = rmsnorm_rows(x2, norm_g[0, 0])
    shared = None
    for l in range(DEPTH):
        g = norm_g[l]
        mem_n = rmsnorm_rows(mem2, g[4])
        mkv = matmul(mem_n, mem_w_kv, l, name="mem_kv").reshape(B, N_MEM, 2 * mq_cols)
        if l < n_a:
            q = matmul(u, a_w_in, l, col0=0, ncols=qa_cols, out_dtype=F32, name="a_q").reshape(B, S, qa_cols)
            k = matmul(u, a_w_in, l, col0=qa_cols, ncols=qa_cols, out_dtype=F32, name="a_k").reshape(B, S, qa_cols)
            v = matmul(u, a_w_in, l, col0=2 * qa_cols, ncols=qa_cols, out_dtype=F32,
                       name="a_v").reshape(B, S, qa_cols)
            mq = matmul(u, a_w_in, l, col0=3 * qa_cols, ncols=mq_cols, name="a_mq").reshape(B, S, mq_cols)
            o_main = dilated_attention(q, k, v)
            o_mem = memory_attention(mq, 0, mkv)
            w_out = a_w_out_b
            li = l
        else:
            li = l - n_a
            if shared is None:
                kvn = rmsnorm_rows(x2, kv_norm_g)
                kvc = matmul(kvn, kv_w3, 0, col0=0, ncols=2 * G * HEAD_DIM, out_dtype=F32,
                             name="kv_cmp").reshape(B, S, 2 * G * HEAD_DIM)
                kvr = matmul(kvn, kv_w3, 0, col0=2 * G * HEAD_DIM, ncols=4 * G * HEAD_DIM,
                             name="kv_rest").reshape(B, S, 4 * G * HEAD_DIM)
                kc = compress_blocks(kvc, 0, cmp_wk1, cmp_wk2, cmp_pe[0])
                vc = compress_blocks(kvc, 1, cmp_wv1, cmp_wv2, cmp_pe[1])
                shared = (kc, vc, kvr)
            qm = matmul(u, b_w_qm, li, name="b_qm").reshape(B, S, qb_cols + mq_cols)
            gates = matmul(u, b_w_gate, li, out_dtype=F32, name="b_gate").reshape(B, S, G * LANES)
            o_main = nsa_attention(qm, gates, *shared)
            o_mem = memory_attention(qm, qb_cols, mkv)
            w_out = b_w_out_b
        x2, u = proj_residual_norm(o_main.reshape(M, -1), o_mem.reshape(M, mq_cols), 0, mq_cols,
                                   w_out, li, x2, g[1], g[2])
        h = ffn_gate_up(u, ffn_w_gu, l)
        gn = norm_g[l + 1, 0] if l + 1 < DEPTH else None
        x2, u = proj_residual_norm(h, None, 0, 0, w_down_b, l, x2, g[3], gn, tm=256, n_sub=1)
    return x2.reshape(B, S, D)
```

```python
import functools
import math

import jax
import jax.numpy as jnp
from jax import lax
from jax.experimental import pallas as pl
from jax.experimental.pallas import tpu as pltpu

BF = jnp.bfloat16
F32 = jnp.float32

D_MODEL = 2048
DEPTH = 4
HEAD_DIM = 128
N_MEM = 256
MEM_HEADS = 4
DIL_GROUPS = ((128, 1), (512, 4), (2048, 16))
A_HEADS_PER_GROUP = 8
A_BLOCK = 128
B_HEADS = 12
B_KV_GROUPS = 4
B_REP = B_HEADS // B_KV_GROUPS
CMP_LEN = 32
CMP_STRIDE = 16
CMP_HIDDEN = 512
SLC_LEN = 64
SLC_TOPK = 16
SEL_FORCE = 1e4
WIN_LEN = 512
D_FF = 5632
EPS = 1e-6

LANES = 128
VMEM_LIMIT = 56 * 1024 * 1024
NEG = -1e30
SCALE = 1.0 / math.sqrt(HEAD_DIM)
LOG2E = math.log2(math.e)
LN2 = math.log(2.0)
SCALE2 = SCALE * LOG2E
NT = (((1,), (1,)), ((), ()))
DIL_GROUP_BLOCKS = 8
STAGE_STRIDE = 4


def _params(*sem):
    return pltpu.CompilerParams(dimension_semantics=sem, vmem_limit_bytes=VMEM_LIMIT)


def _masked_softmax(s, valid):
    s = jnp.where(valid, s, NEG)
    m = jnp.max(s, axis=-1, keepdims=True)
    e = jnp.where(valid, jnp.exp(s - m), 0.0)
    den = jnp.maximum(jnp.sum(e, axis=-1, keepdims=True), 1e-30)
    return e * (1.0 / den)


def _rms(x, g):
    return x * lax.rsqrt(jnp.mean(x * x, axis=-1, keepdims=True) + EPS) * g


def _iota(shape, dim):
    return lax.broadcasted_iota(jnp.int32, shape, dim)


def _rmsnorm_kernel(x_ref, g_ref, o_ref):
    o_ref[...] = _rms(x_ref[...], g_ref[...]).astype(o_ref.dtype)


def rmsnorm_rows(x, g, tm=512):
    M, D = x.shape
    tm = min(tm, M)
    return pl.pallas_call(
        _rmsnorm_kernel,
        grid=(M // tm,),
        in_specs=[pl.BlockSpec((tm, D), lambda i: (i, 0)),
                  pl.BlockSpec((1, D), lambda i: (0, 0))],
        out_specs=pl.BlockSpec((tm, D), lambda i: (i, 0)),
        out_shape=jax.ShapeDtypeStruct((M, D), BF),
        compiler_params=_params("arbitrary"),
        name="rmsnorm",
    )(x, g.reshape(1, D))


def _mem_kv_kernel(x_ref, g_ref, w_ref, o_ref):
    xn = _rms(x_ref[...], g_ref[...]).astype(BF)
    o_ref[...] = jnp.dot(xn, w_ref[...].astype(BF), preferred_element_type=F32).astype(o_ref.dtype)


def memory_kv_all_layers(mem2, g, w, tn=512):
    Mm, D = mem2.shape
    L, _, N = w.shape
    return pl.pallas_call(
        _mem_kv_kernel,
        grid=(L, N // tn),
        in_specs=[pl.BlockSpec((Mm, D), lambda l, j: (0, 0)),
                  pl.BlockSpec((None, 1, D), lambda l, j: (l, 0, 0)),
                  pl.BlockSpec((None, D, tn), lambda l, j: (l, 0, j))],
        out_specs=pl.BlockSpec((None, Mm, tn), lambda l, j: (l, 0, j)),
        out_shape=jax.ShapeDtypeStruct((L, Mm, N), BF),
        compiler_params=_params("arbitrary", "arbitrary"),
        name="mem_kv",
    )(mem2, g.reshape(L, 1, D), w)


def _mm_kernel(has_scale, *refs):
    if has_scale:
        x_ref, w_ref, s_ref, o_ref, wbf_ref = refs
    else:
        x_ref, w_ref, o_ref, wbf_ref = refs
    j = pl.program_id(1)

    @pl.when(pl.program_id(0) == 0)
    def _():
        wbf_ref[j] = w_ref[...].astype(BF)

    acc = jnp.dot(x_ref[...], wbf_ref[j], preferred_element_type=F32)
    if has_scale:
        acc = acc * s_ref[...]
    o_ref[...] = acc.astype(o_ref.dtype)


def matmul(x, w, layer, *, col0=0, ncols=None, tm=1024, out_dtype=BF, col_scale=None, name="matmul"):
    M, K = x.shape
    ncols = w.shape[2] - col0 if ncols is None else ncols
    tn = 1024 if (ncols % 1024 == 0 and col0 % 1024 == 0) else 512
    tm = min(tm, M)
    assert col0 % tn == 0 and ncols % tn == 0 and M % tm == 0
    c0 = col0 // tn
    in_specs = [pl.BlockSpec((tm, K), lambda i, j: (i, 0)),
                pl.BlockSpec((None, K, tn), lambda i, j: (layer, 0, jnp.where(i == 0, j, 0) + c0))]
    args = [x, w]
    if col_scale is not None:
        in_specs.append(pl.BlockSpec((1, tn), lambda i, j: (0, j)))
        args.append(col_scale.reshape(1, ncols))
    return pl.pallas_call(
        functools.partial(_mm_kernel, col_scale is not None),
        grid=(M // tm, ncols // tn),
        in_specs=in_specs,
        out_specs=pl.BlockSpec((tm, tn), lambda i, j: (i, j)),
        out_shape=jax.ShapeDtypeStruct((M, ncols), out_dtype),
        scratch_shapes=[pltpu.VMEM((ncols // tn, K, tn), BF)],
        compiler_params=_params("arbitrary", "arbitrary"),
        name=name,
    )(*args)


def _gu_kernel(x_ref, wg_ref, wu_ref, o_ref, wgb_ref, wub_ref):
    @pl.when(pl.program_id(1) == 0)
    def _():
        wgb_ref[...] = wg_ref[...].astype(BF)
        wub_ref[...] = wu_ref[...].astype(BF)

    x = x_ref[...]
    g = jnp.dot(x, wgb_ref[...], preferred_element_type=F32)
    u = jnp.dot(x, wub_ref[...], preferred_element_type=F32)
    o_ref[...] = (g * (1.0 / (1.0 + jnp.exp(-g))) * u).astype(o_ref.dtype)


def ffn_gate_up(x, w, layer, tn=512, tm=1024):
    M, K = x.shape
    nj = D_FF // tn
    return pl.pallas_call(
        _gu_kernel,
        grid=(nj, M // tm),
        in_specs=[pl.BlockSpec((tm, K), lambda j, i: (i, 0)),
                  pl.BlockSpec((None, K, tn), lambda j, i: (layer, 0, j)),
                  pl.BlockSpec((None, K, tn), lambda j, i: (layer, 0, j + nj))],
        out_specs=pl.BlockSpec((tm, tn), lambda j, i: (i, j)),
        out_shape=jax.ShapeDtypeStruct((M, D_FF), BF),
        scratch_shapes=[pltpu.VMEM((K, tn), BF), pltpu.VMEM((K, tn), BF)],
        compiler_params=_params("arbitrary", "arbitrary"),
        name="ffn_gate_up",
    )(x, w, w)


def _proj_res_kernel(has_x2, has_next, n_sub, *refs):
    refs = list(refs)
    x1_ref = refs.pop(0)
    x2_ref = refs.pop(0) if has_x2 else None
    w_ref, xres_ref, g_ref = refs.pop(0), refs.pop(0), refs.pop(0)
    gn_ref = refs.pop(0) if has_next else None
    xo_ref = refs.pop(0)
    uo_ref = refs.pop(0) if has_next else None
    k1 = x1_ref.shape[1]
    sub = xo_ref.shape[0] // n_sub
    for h in range(n_sub):
        rows = slice(h * sub, (h + 1) * sub)
        acc = jnp.dot(x1_ref[rows, :], w_ref[:k1, :], preferred_element_type=F32)
        if has_x2:
            acc = acc + jnp.dot(x2_ref[rows, :], w_ref[k1:, :], preferred_element_type=F32)
        xn = xres_ref[rows, :] + _rms(acc, g_ref[...])
        xo_ref[rows, :] = xn
        if has_next:
            uo_ref[rows, :] = _rms(xn, gn_ref[...]).astype(uo_ref.dtype)


def proj_residual_norm(x1, x2, x2_col0, k2, w, layer, xres, g, gn, tm=512, n_sub=2):
    M, K1 = x1.shape
    K, N = w.shape[1], w.shape[2]
    has_x2, has_next = x2 is not None, gn is not None
    assert K == K1 + (k2 if has_x2 else 0)
    row = lambda i: (i, 0)
    in_specs = [pl.BlockSpec((tm, K1), row)]
    args = [x1]
    if has_x2:
        assert x2_col0 % k2 == 0
        in_specs.append(pl.BlockSpec((tm, k2), lambda i: (i, x2_col0 // k2)))
        args.append(x2)
    in_specs += [pl.BlockSpec((None, K, N), lambda i: (layer, 0, 0), pipeline_mode=pl.Buffered(1)),
                 pl.BlockSpec((tm, N), row),
                 pl.BlockSpec((1, N), lambda i: (0, 0))]
    args += [w, xres, g.reshape(1, N)]
    out_specs = [pl.BlockSpec((tm, N), row)]
    out_shape = [jax.ShapeDtypeStruct((M, N), F32)]
    if has_next:
        in_specs.append(pl.BlockSpec((1, N), lambda i: (0, 0)))
        args.append(gn.reshape(1, N))
        out_specs.append(pl.BlockSpec((tm, N), row))
        out_shape.append(jax.ShapeDtypeStruct((M, N), BF))
    res = pl.pallas_call(
        functools.partial(_proj_res_kernel, has_x2, has_next, n_sub),
        grid=(M // tm,),
        in_specs=in_specs,
        out_specs=out_specs,
        out_shape=out_shape,
        compiler_params=_params("arbitrary"),
        name="proj_residual_norm",
    )(*args)
    return (res[0], res[1]) if has_next else (res[0], None)


def _divmod_pow2(n, d):
    assert d & (d - 1) == 0
    return lax.shift_right_logical(n, d.bit_length() - 1), n & (d - 1)


def _dil_kernel(*refs):
    ng = len(DIL_GROUPS)
    q_refs, k_refs, v_refs = refs[:ng], refs[ng:2 * ng], refs[2 * ng:3 * ng]
    o_ref, o_scr, l_scr, stage_scr, ostage_scr = refs[3 * ng:]
    S = o_ref.shape[1]
    blk = A_BLOCK
    n_heads = ng * A_HEADS_PER_GROUP
    j = pl.program_id(1)
    for gi, (w, d) in enumerate(DIL_GROUPS):
        q_ref, k_ref, v_ref = q_refs[gi], k_refs[gi], v_refs[gi]
        nqb = S // d // blk
        max_dist = w // d
        has_prev = nqb > 1
        nk = 2 * blk if has_prev else blk
        koff = blk if has_prev else 0
        head = (j + (gi * A_HEADS_PER_GROUP + 1)).astype(F32)
        slope = jnp.exp2(jnp.zeros((1, 1), F32) + head * (-8.0 / n_heads))
        kidx = _iota((blk, nk), 1)
        dist = _iota((blk, nk), 0) + koff - kidx
        band = (dist >= 0) & (dist <= max_dist)
        bias = jnp.where(band, (kidx - koff).astype(F32) * (slope * (d * LOG2E)), NEG)
        bias_first = jnp.where(kidx >= koff, bias, NEG)
        row_term = _iota((blk, 1), 0).astype(F32) * (slope * (-float(d)))

        staged = d > STAGE_STRIDE
        if staged:
            assert d % STAGE_STRIDE == 0 and d // STAGE_STRIDE <= STAGE_STRIDE
            quarter = S // STAGE_STRIDE
            for t_idx, ref in enumerate((q_ref, k_ref, v_ref)):
                for r1 in range(STAGE_STRIDE):
                    stage_scr[t_idx, r1 * quarter:(r1 + 1) * quarter, :] = ref[0, pl.ds(r1, quarter, stride=STAGE_STRIDE), :]

        def out_rows(r, i, d=d):
            start = r + i * (d * blk)
            return pl.ds(start, blk, stride=d) if d > 1 else pl.ds(start, blk)

        def staged_rows(r, i, d=d):
            s2 = d // STAGE_STRIDE
            r2, r1 = _divmod_pow2(r, STAGE_STRIDE)
            return pl.ds(r1 * (S // STAGE_STRIDE) + r2 + i * (s2 * blk), blk, stride=s2)

        def load(t_idx, ref, r, i, staged=staged):
            return stage_scr[t_idx, staged_rows(r, i), :] if staged else ref[0, out_rows(r, i), :]

        def body(n0, carry, gi=gi, d=d, has_prev=has_prev, q_ref=q_ref, k_ref=k_ref, v_ref=v_ref,
                 bias=bias, bias_first=bias_first, row_term=row_term, out_rows=out_rows, load=load,
                 staged=staged, staged_rows=staged_rows):
            ri = [_divmod_pow2(n0 * DIL_GROUP_BLOCKS + g, d)[::-1] for g in range(DIL_GROUP_BLOCKS)]
            qs = [load(0, q_ref, r, i).astype(BF) for r, i in ri]
            if has_prev:
                ips = [jnp.maximum(i - 1, 0) for _, i in ri]
                ks = [jnp.concatenate([load(1, k_ref, r, ip), load(1, k_ref, r, i)], axis=0).astype(BF)
                      for (r, i), ip in zip(ri, ips)]
                vs = [jnp.concatenate([load(2, v_ref, r, ip), load(2, v_ref, r, i)], axis=0).astype(BF)
                      for (r, i), ip in zip(ri, ips)]
                bs = [jnp.where(i > 0, bias, bias_first) for _, i in ri]
            else:
                ks = [load(1, k_ref, r, i).astype(BF) for r, i in ri]
                vs = [load(2, v_ref, r, i).astype(BF) for r, i in ri]
                bs = [bias] * len(ri)
            ss = [lax.dot_general(q, k, NT, preferred_element_type=F32) + b for q, k, b in zip(qs, ks, bs)]
            ms = [jnp.max(s, axis=-1, keepdims=True) for s in ss]
            ps = [jnp.exp2(s - m) for s, m in zip(ss, ms)]
            ls = [jnp.sum(p, axis=-1, keepdims=True) for p in ps]
            os_ = [jnp.dot(p.astype(BF), v, preferred_element_type=F32) * (1.0 / l) for p, v, l in zip(ps, vs, ls)]
            for (r, i), o, m, l in zip(ri, os_, ms, ls):
                lse_b = jnp.broadcast_to((m + jnp.log2(l)) * LN2 + row_term, (blk, HEAD_DIM))
                if staged:
                    rows = staged_rows(r, i)
                    ostage_scr[0, rows, :] = o
                    ostage_scr[1, rows, :] = lse_b
                else:
                    rows = out_rows(r, i)
                    o_scr[gi, rows, :] = o
                    l_scr[gi, rows, :] = lse_b
            return carry

        assert (d * nqb) % DIL_GROUP_BLOCKS == 0
        lax.fori_loop(0, d * nqb // DIL_GROUP_BLOCKS, body, 0)
        if staged:
            for r1 in range(STAGE_STRIDE):
                dst = pl.ds(r1, quarter, stride=STAGE_STRIDE)
                o_scr[gi, dst, :] = ostage_scr[0, r1 * quarter:(r1 + 1) * quarter, :]
                l_scr[gi, dst, :] = ostage_scr[1, r1 * quarter:(r1 + 1) * quarter, :]

    tr = 256
    for c in range(S // tr):
        rows = slice(c * tr, (c + 1) * tr)
        ls = [l_scr[gi, rows, :] for gi in range(ng)]
        m = functools.reduce(jnp.maximum, ls)
        es = [jnp.exp(x - m) for x in ls]
        num = sum(e * o_scr[gi, rows, :] for gi, e in enumerate(es))
        o_ref[0, rows, :] = (num * (1.0 / sum(es))).astype(o_ref.dtype)


def dilated_attention(q, k, v):
    B, S, _ = q.shape
    ng, Hg = len(DIL_GROUPS), A_HEADS_PER_GROUP
    for w, d in DIL_GROUPS:
        assert w // d <= A_BLOCK and S % (d * A_BLOCK) == 0
    specs = [pl.BlockSpec((1, S, HEAD_DIM), lambda b, j, gi=gi: (b, 0, gi * Hg + j)) for gi in range(ng)]
    return pl.pallas_call(
        _dil_kernel,
        grid=(B, Hg),
        in_specs=specs * 3,
        out_specs=pl.BlockSpec((1, S, HEAD_DIM), lambda b, j: (b, 0, j)),
        out_shape=jax.ShapeDtypeStruct((B, S, Hg * HEAD_DIM), BF),
        scratch_shapes=[pltpu.VMEM((ng, S, HEAD_DIM), F32), pltpu.VMEM((ng, S, HEAD_DIM), F32),
                        pltpu.VMEM((3, S, HEAD_DIM), F32), pltpu.VMEM((2, S, HEAD_DIM), F32)],
        compiler_params=_params("arbitrary", "arbitrary"),
        name="dilated_attention",
    )(*([q] * ng + [k] * ng + [v] * ng))


def _mem_attn_kernel(q_ref, kv_ref, o_ref):
    for h in range(MEM_HEADS):
        c = slice(h * HEAD_DIM, (h + 1) * HEAD_DIM)
        k = kv_ref[0, :, c]
        v = kv_ref[0, :, MEM_HEADS * HEAD_DIM + h * HEAD_DIM:MEM_HEADS * HEAD_DIM + (h + 1) * HEAD_DIM]
        s = lax.dot_general(q_ref[0, :, c], k, NT, preferred_element_type=F32) * SCALE
        m = jnp.max(s, axis=-1, keepdims=True)
        e = jnp.exp(s - m)
        p = e * (1.0 / jnp.sum(e, axis=-1, keepdims=True))
        o_ref[0, :, c] = jnp.dot(p.astype(BF), v, preferred_element_type=F32).astype(o_ref.dtype)


def memory_attention(q, q_col0, kv, layer, tq=512):
    B, S, _ = q.shape
    w = MEM_HEADS * HEAD_DIM
    cb = q_col0 // w
    return pl.pallas_call(
        _mem_attn_kernel,
        grid=(B, S // tq),
        in_specs=[pl.BlockSpec((1, tq, w), lambda b, i: (b, i, cb)),
                  pl.BlockSpec((None, 1, N_MEM, 2 * w), lambda b, i: (layer, b, 0, 0))],
        out_specs=pl.BlockSpec((1, tq, w), lambda b, i: (b, i, 0)),
        out_shape=jax.ShapeDtypeStruct((B, S, w), BF),
        compiler_params=_params("arbitrary", "arbitrary"),
        name="memory_attention",
    )(q, kv)


def _gelu_tanh(x):
    return 0.5 * x * (1.0 + jnp.tanh(math.sqrt(2.0 / math.pi) * (x + 0.044715 * (x * x * x))))


def _compress_kernel(out_scale, x_ref, w1_ref, w2_ref, pe_ref, o_ref, w1b_ref):
    @pl.when((pl.program_id(0) == 0) & (pl.program_id(1) == 0))
    def _():
        w1b_ref[...] = w1_ref[...].astype(BF)

    nchunk = o_ref.shape[1]
    top = jnp.zeros((nchunk, CMP_HIDDEN), F32)
    bot = jnp.zeros((nchunk, CMP_HIDDEN), F32)
    for l in range(CMP_STRIDE):
        xl = x_ref[0, pl.ds(l, nchunk, stride=CMP_STRIDE), :]
        lo, hi = l, CMP_STRIDE + l
        top += jnp.dot((xl + pe_ref[lo:lo + 1, :]).astype(BF),
                       w1b_ref[lo * HEAD_DIM:(lo + 1) * HEAD_DIM, :], preferred_element_type=F32)
        bot += jnp.dot((xl + pe_ref[hi:hi + 1, :]).astype(BF),
                       w1b_ref[hi * HEAD_DIM:(hi + 1) * HEAD_DIM, :], preferred_element_type=F32)
    hid = _gelu_tanh(top + pltpu.roll(bot, nchunk - 1, 0))
    out = jnp.dot(hid.astype(BF), w2_ref[...].astype(BF), preferred_element_type=F32)
    o_ref[0] = (out * out_scale).astype(o_ref.dtype)


def compress_blocks(kvc, sec, w1, w2, pe, out_scale=1.0):
    B, S, C = kvc.shape
    assert CMP_LEN == 2 * CMP_STRIDE
    G = B_KV_GROUPS
    nchunk = S // CMP_STRIDE
    return pl.pallas_call(
        functools.partial(_compress_kernel, out_scale),
        grid=(B, G),
        in_specs=[pl.BlockSpec((1, S, HEAD_DIM), lambda b, g: (b, 0, sec * G + g)),
                  pl.BlockSpec(w1.shape, lambda b, g: (0, 0)),
                  pl.BlockSpec(w2.shape, lambda b, g: (0, 0)),
                  pl.BlockSpec(pe.shape, lambda b, g: (0, 0))],
        out_specs=pl.BlockSpec((1, nchunk, HEAD_DIM), lambda b, g: (b, 0, g)),
        out_shape=jax.ShapeDtypeStruct((B, nchunk, G * HEAD_DIM), BF),
        scratch_shapes=[pltpu.VMEM(w1.shape, BF)],
        compiler_params=_params("arbitrary", "arbitrary"),
        name=f"compress_{sec}",
    )(kvc, w1, w2, pe)


ONES_ROWS = 16
AUG_K = LANES
NEG_ROWS = 16


def _key_aug(nk, bps):
    idx = _iota((nk, AUG_K), 0)
    lane = _iota((nk, AUG_K), 1)
    piece = jnp.where((lane - NEG_ROWS) % 2 == 0, (idx // 256) * 256, idx % 256)
    val = jnp.where(lane < bps, (idx // SLC_LEN == lane).astype(jnp.int32),
                    jnp.where((lane >= NEG_ROWS) & (lane < NEG_ROWS + 6), piece, 0))
    return val.astype(F32).astype(BF)


def _offset_masks(n, nk, tq, lo, hi):
    dist = (_iota((n, nk, tq), 0) * tq + _iota((n, nk, tq), 2)) - _iota((n, nk, tq), 1)
    return jnp.where((dist >= lo) & (dist <= hi), 0.0, NEG).astype(F32)


def _t_bf(x):
    return x.astype(F32).T.astype(BF)


def _nsa_kernel(tq, q_ref, gate_ref, kc_ref, vc_ref, ks_ref, vs_ref, kw_ref, vw_ref,
                sl_ref, kaug_w_ref, kaug_s_ref, wmask_ref, cmask_ref, o_ref,
                vst_ref, vwt_ref, qt_ref, ocmp_ref, neg_ref, raw_a_ref, raw_b_ref, pc_ref):
    S = ks_ref.shape[1]
    ncmp = kc_ref.shape[1]
    nsel = S // SLC_LEN
    i = pl.program_id(2)

    t0 = i * tq
    tw = B_REP * tq
    nt = S // tq
    n_back = -(-(WIN_LEN - 1) // tq)
    nwin = (n_back + 1) * tq
    kcs = vst_ref.shape[2]
    bps = kcs // SLC_LEN

    def per_head(fn, n=B_REP):
        return jnp.concatenate([fn(r) for r in range(n)], axis=1)

    def bf_exact(x):
        return x.astype(BF).astype(F32)

    @pl.when(i == 0)
    def _():
        for src_ref, dst_ref in ((vs_ref, vst_ref), (vw_ref, vwt_ref)):
            n, _, w = dst_ref.shape
            ones = jnp.ones((ONES_ROWS, w), BF)
            for c in range(n):
                dst_ref[c] = jnp.concatenate([_t_bf(src_ref[0, c * w:(c + 1) * w, :]), ones], axis=0)
        vct = _t_bf(vc_ref[0])
        for t in range(nt):
            for r in range(B_REP):
                qt_ref[t, r] = _t_bf(q_ref[0, t * tq:(t + 1) * tq, r * HEAD_DIM:(r + 1) * HEAD_DIM])
        q_w = jnp.concatenate([qt_ref[t, r] for r in range(B_REP) for t in range(nt)], axis=1)
        sl2_w = per_head(lambda r: jnp.broadcast_to(sl_ref[0, 0:1, r:r + 1], (1, S))) * LOG2E
        pos = _iota((1, S), 1)
        pos_w = per_head(lambda r: pos)

        dist_c = pos_w - (_iota((ncmp, 1), 0) * CMP_STRIDE + (CMP_LEN - 1))
        s = (jnp.dot(kc_ref[0], q_w, preferred_element_type=F32)
             + jnp.where(dist_c >= 0, dist_c.astype(F32) * (-sl2_w), NEG))
        e = jnp.exp2(s - jnp.max(s, axis=0, keepdims=True))
        has_block = (pos_w >= CMP_LEN - 1).astype(F32)
        p = e * (has_block / jnp.maximum(jnp.sum(e, axis=0, keepdims=True), 1e-30))
        o_cmp_w = jnp.dot(vct, p.astype(BF), preferred_element_type=F32)
        for t in range(nt):
            for r in range(B_REP):
                ocmp_ref[t, r] = o_cmp_w[:, r * S + t * tq:r * S + (t + 1) * tq]

        ratio = SLC_LEN // CMP_STRIDE
        assert SLC_LEN % CMP_STRIDE == 0 and CMP_LEN <= SLC_LEN and ncmp >= nsel * ratio
        p_heads = sum(p[:, r * S:(r + 1) * S] for r in range(B_REP))
        for lt in range(S // LANES):
            pc_ref[lt] = p_heads[:, lt * LANES:(lt + 1) * LANES]
        imp = jnp.zeros((nsel, S), F32)
        for o in range(1 - (-(-CMP_LEN // CMP_STRIDE)), ratio):
            overlap = max(min(o * CMP_STRIDE + CMP_LEN, SLC_LEN) - max(o * CMP_STRIDE, 0), 0)
            if overlap == 0:
                continue
            part = jnp.concatenate([pc_ref[lt, pl.ds(o % ratio, nsel, stride=ratio), :]
                                    for lt in range(S // LANES)], axis=1)
            if o < 0:
                part = jnp.where(_iota((nsel, S), 0) == 0, 0.0, pltpu.roll(part, 1, 0))
            imp = imp + (overlap / CMP_LEN) * part

        jb = _iota((nsel, S), 0)
        cur = pos // SLC_LEN
        forced = (jb == 0) | (jb == cur) | (jb == cur - 1)
        imp = jnp.where(forced, SEL_FORCE, jnp.where(jb > cur, -SEL_FORCE, imp))
        beaten_by = jnp.zeros((nsel, S), jnp.int32)
        for j in range(nsel):
            row = imp[j:j + 1, :]
            beats = (row > imp) | ((row == imp) & (jb > j))
            beaten_by = beaten_by + beats.astype(jnp.int32)
        neg_w = jnp.where(beaten_by >= min(SLC_TOPK, nsel), NEG, 0.0)
        for t in range(nt):
            neg_ref[t] = neg_w[:, t * tq:(t + 1) * tq]

    q_all = per_head(lambda r: qt_ref[i, r])
    o_cmp = per_head(lambda r: ocmp_ref[i, r])
    sl_row = per_head(lambda r: jnp.broadcast_to(sl_ref[0, 0:1, r:r + 1], (1, tq)))
    sl2_row = sl_row * LOG2E

    c1 = bf_exact(sl2_row)
    c2 = bf_exact(sl2_row - c1)
    c3 = bf_exact(sl2_row - c1 - c2)
    slope_rows = jnp.concatenate([c1, c1, c2, c2, c3, c3, jnp.zeros((AUG_K - NEG_ROWS - 6, tw), F32)],
                                 axis=0).astype(BF)

    def scores(k_c, k_aug, neg_rows):
        q_full = jnp.concatenate([q_all, neg_rows, slope_rows], axis=0)
        return jnp.dot(jnp.concatenate([k_c, k_aug], axis=1), q_full, preferred_element_type=F32)

    def neg_rows_of(n):
        rows = neg_ref[i, pl.ds(pl.multiple_of(n * bps, bps), bps), :]
        return jnp.concatenate([per_head(lambda r: rows), jnp.zeros((NEG_ROWS - bps, tw), F32)], axis=0).astype(BF)

    no_rows = jnp.zeros((NEG_ROWS, tw), BF)
    c0 = jnp.maximum(i - n_back, 0)
    wbase = pl.multiple_of(c0 * tq, tq)
    raw_w = scores(kw_ref[0, pl.ds(wbase, nwin), :], kaug_w_ref[...], no_rows)
    k_aug = kaug_s_ref[...]
    n_last = (t0 + tq - 1) // kcs
    raw_d = scores(ks_ref[0, pl.ds(pl.multiple_of(n_last * kcs, kcs), kcs), :], k_aug, no_rows)

    def softmax_step(raw, vt_c, mask, c_off, st):
        m, acc = st
        s = raw if mask is None else raw + per_head(lambda r: mask)
        m_new = jnp.maximum(m, jnp.max(s, axis=0, keepdims=True) + c_off)
        p = jnp.exp2(s - (m_new - c_off))
        acc = jnp.exp2(m - m_new) * acc + jnp.dot(vt_c, p.astype(BF), preferred_element_type=F32)
        return m_new, acc

    def finish(st):
        _, acc = st
        return acc[:HEAD_DIM] * (1.0 / acc[HEAD_DIM:HEAD_DIM + 1])

    init = (jnp.full((1, tw), NEG, F32), jnp.zeros((HEAD_DIM + ONES_ROWS, tw), F32))
    no_off = jnp.zeros((1, 1), F32)

    vwt = jnp.concatenate([vwt_ref[c0 + j] for j in range(n_back + 1)], axis=1)
    o_win = finish(softmax_step(raw_w, vwt, wmask_ref[i - c0], no_off, init))

    def slab_scores(n):
        return scores(ks_ref[0, pl.ds(pl.multiple_of(n * kcs, kcs), kcs), :], k_aug, neg_rows_of(n))

    raw_a_ref[...] = slab_scores(0)
    sel_rows = [jnp.broadcast_to(neg_ref[i, pl.ds(n_last * bps + b, 1), :], (SLC_LEN, tq)) for b in range(bps)]
    mask_d = jnp.concatenate(sel_rows, axis=0) + cmask_ref[i - n_last * (kcs // tq)]
    st = softmax_step(raw_d, vst_ref[n_last], mask_d, (n_last * kcs - t0).astype(F32) * sl2_row, init)

    def sel_slab(n, st, src_ref, dst_ref):
        dst_ref[...] = slab_scores(jnp.minimum(n + 1, n_last))
        return softmax_step(src_ref[...], vst_ref[n], None, (n * kcs - t0).astype(F32) * sl2_row, st)

    def sel_pair(j, st):
        st = sel_slab(2 * j, st, raw_a_ref, raw_b_ref)
        return lax.cond(2 * j + 1 < n_last, lambda s: sel_slab(2 * j + 1, s, raw_b_ref, raw_a_ref),
                        lambda s: s, st)

    o_slc = finish(lax.fori_loop(0, (n_last + 1) // 2, sel_pair, st))

    gate_t = (1.0 / (1.0 + jnp.exp(-gate_ref[0]))).T
    for r in range(B_REP):
        c = slice(r * tq, (r + 1) * tq)
        o_t = (gate_t[3 * r:3 * r + 1] * o_cmp[:, c] + gate_t[3 * r + 1:3 * r + 2] * o_slc[:, c]
               + gate_t[3 * r + 2:3 * r + 3] * o_win[:, c])
        o_ref[0, :, r * HEAD_DIM:(r + 1) * HEAD_DIM] = o_t.T.astype(o_ref.dtype)


def nsa_attention(qp, gates, kc, vc, kv, tq=256):
    B, S, _ = qp.shape
    G = B_KV_GROUPS
    gq = B_REP * HEAD_DIM
    ncmp = kc.shape[1]
    kcs = 2 * tq
    nsel = S // SLC_LEN
    assert kcs % SLC_LEN == 0 and S % kcs == 0 and S >= (-(-(WIN_LEN - 1) // tq) + 1) * tq
    slopes = 2.0 ** (-8.0 * jnp.arange(1, B_HEADS + 1, dtype=F32) / B_HEADS)
    sl = jnp.zeros((G, 8, LANES), F32).at[:, :, :B_REP].set(
        jnp.broadcast_to(slopes.reshape(G, 1, B_REP), (G, 8, B_REP)))
    kvspec = lambda sec: pl.BlockSpec((1, S, HEAD_DIM), lambda b, g, i: (b, 0, sec * G + g))
    cspec = pl.BlockSpec((1, ncmp, HEAD_DIM), lambda b, g, i: (b, 0, g))
    vt_rows = HEAD_DIM + ONES_ROWS
    n_back = -(-(WIN_LEN - 1) // tq)
    nwin = (n_back + 1) * tq
    bps = kcs // SLC_LEN
    assert bps <= NEG_ROWS and NEG_ROWS + 6 <= AUG_K
    consts = [_key_aug(nwin, bps), _key_aug(kcs, bps), _offset_masks(n_back + 1, nwin, tq, 0, WIN_LEN - 1),
              _offset_masks(kcs // tq, kcs, tq, 0, S)]
    whole = lambda a: pl.BlockSpec(a.shape, lambda b, g, i, nd=a.ndim: (0,) * nd)
    return pl.pallas_call(
        functools.partial(_nsa_kernel, tq),
        grid=(B, G, S // tq),
        in_specs=[pl.BlockSpec((1, S, gq), lambda b, g, i: (b, 0, g)),
                  pl.BlockSpec((1, tq, LANES), lambda b, g, i: (b, i, g)),
                  cspec, cspec, kvspec(0), kvspec(1), kvspec(2), kvspec(3),
                  pl.BlockSpec((1, 8, LANES), lambda b, g, i: (g, 0, 0))] + [whole(a) for a in consts],
        out_specs=pl.BlockSpec((1, tq, gq), lambda b, g, i: (b, i, g)),
        out_shape=jax.ShapeDtypeStruct((B, S, B_HEADS * HEAD_DIM), BF),
        scratch_shapes=[pltpu.VMEM((S // kcs, vt_rows, kcs), BF), pltpu.VMEM((S // tq, vt_rows, tq), BF),
                        pltpu.VMEM((S // tq, B_REP, HEAD_DIM, tq), BF), pltpu.VMEM((S // tq, B_REP, HEAD_DIM, tq), F32),
                        pltpu.VMEM((S // tq, nsel, tq), F32),
                        pltpu.VMEM((kcs, B_REP * tq), F32), pltpu.VMEM((kcs, B_REP * tq), F32),
                        pltpu.VMEM((S // LANES, ncmp, LANES), F32)],
        compiler_params=_params("arbitrary", "arbitrary", "arbitrary"),
        name="nsa_attention",
    )(qp, gates, kc, vc, kv, kv, kv, kv, sl, *consts)


def kernel(x, mem, norm_g, a_w_in, a_w_out, b_w_in, b_w_out, mem_w_kv, ffn_w_gu, ffn_w_down,
           kv_norm_g, kv_w, cmp_pe, cmp_wk1, cmp_wk2, cmp_wv1, cmp_wv2):
    B, S, D = x.shape
    M = B * S
    n_a = DEPTH // 2
    qa_cols = len(DIL_GROUPS) * A_HEADS_PER_GROUP * HEAD_DIM
    qb_cols = B_HEADS * HEAD_DIM
    mq_cols = MEM_HEADS * HEAD_DIM
    n_gate = 3 * B_HEADS
    G = B_KV_GROUPS

    b_w_qm = jnp.concatenate([b_w_in[:, :, :qb_cols], b_w_in[:, :, qb_cols + n_gate:]], axis=-1)
    b_w_gate = jnp.pad(b_w_in[:, :, qb_cols:qb_cols + n_gate].reshape(-1, D, G, n_gate // G),
                       ((0, 0), (0, 0), (0, 0), (0, LANES - n_gate // G))).reshape(-1, D, G * LANES)
    a_w_out_b = a_w_out.astype(BF)
    b_w_out_b = b_w_out.astype(BF)
    w_down_b = ffn_w_down.astype(BF)
    kv_w3 = kv_w[None]

    x2 = x.reshape(M, D)
    mkv = memory_kv_all_layers(mem.reshape(B * N_MEM, D), norm_g[:, 4], mem_w_kv).reshape(
        DEPTH, B, N_MEM, 2 * mq_cols)
    u = rmsnorm_rows(x2, norm_g[0, 0])
    shared = None
    for l in range(DEPTH):
        g = norm_g[l]
        if l < n_a:
            q = matmul(u, a_w_in, l, col0=0, ncols=qa_cols, out_dtype=F32, name="a_q").reshape(B, S, qa_cols)
            k = matmul(u, a_w_in, l, col0=qa_cols, ncols=qa_cols, out_dtype=F32,
                       col_scale=jnp.full((qa_cols,), SCALE2, F32), name="a_k").reshape(B, S, qa_cols)
            v = matmul(u, a_w_in, l, col0=2 * qa_cols, ncols=qa_cols, out_dtype=F32,
                       name="a_v").reshape(B, S, qa_cols)
            mq = matmul(u, a_w_in, l, col0=3 * qa_cols, ncols=mq_cols, name="a_mq").reshape(B, S, mq_cols)
            o_main = dilated_attention(q, k, v)
            o_mem = memory_attention(mq, 0, mkv, l)
            w_out = a_w_out_b
            li = l
        else:
            li = l - n_a
            if shared is None:
                kvn = rmsnorm_rows(x2, kv_norm_g)
                kvc = matmul(kvn, kv_w3, 0, col0=0, ncols=2 * G * HEAD_DIM, out_dtype=F32,
                             name="kv_cmp").reshape(B, S, 2 * G * HEAD_DIM)
                sec = G * HEAD_DIM
                key_scale = jnp.tile(jnp.concatenate([jnp.full((sec,), SCALE2, F32), jnp.ones((sec,), F32)]), 2)
                kvr = matmul(kvn, kv_w3, 0, col0=2 * G * HEAD_DIM, ncols=4 * G * HEAD_DIM, col_scale=key_scale,
                             name="kv_rest").reshape(B, S, 4 * G * HEAD_DIM)
                kc = compress_blocks(kvc, 0, cmp_wk1, cmp_wk2, cmp_pe[0], out_scale=SCALE2)
                vc = compress_blocks(kvc, 1, cmp_wv1, cmp_wv2, cmp_pe[1])
                shared = (kc, vc, kvr)
            qm = matmul(u, b_w_qm, li, name="b_qm").reshape(B, S, qb_cols + mq_cols)
            gates = matmul(u, b_w_gate, li, out_dtype=F32, name="b_gate").reshape(B, S, G * LANES)
            o_main = nsa_attention(qm, gates, *shared)
            o_mem = memory_attention(qm, qb_cols, mkv, l)
            w_out = b_w_out_b
        x2, u = proj_residual_norm(o_main.reshape(M, -1), o_mem.reshape(M, mq_cols), 0, mq_cols,
                                   w_out, li, x2, g[1], g[2])
        h = ffn_gate_up(u, ffn_w_gu, l)
        gn = norm_g[l + 1, 0] if l + 1 < DEPTH else None
        x2, u = proj_residual_norm(h, None, 0, 0, w_down_b, l, x2, g[3], gn, tm=256, n_sub=2)
    return x2.reshape(B, S, D)
```

```python
import functools
import math

import jax
import jax.numpy as jnp
from jax import lax
from jax.experimental import pallas as pl
from jax.experimental.pallas import tpu as pltpu

BF = jnp.bfloat16
F32 = jnp.float32

D_MODEL = 2048
DEPTH = 4
HEAD_DIM = 128
N_MEM = 256
MEM_HEADS = 4
DIL_GROUPS = ((128, 1), (512, 4), (2048, 16))
A_HEADS_PER_GROUP = 8
A_BLOCK = 128
B_HEADS = 12
B_KV_GROUPS = 4
B_REP = B_HEADS // B_KV_GROUPS
CMP_LEN = 32
CMP_STRIDE = 16
CMP_HIDDEN = 512
SLC_LEN = 64
SLC_TOPK = 16
SEL_FORCE = 1e4
WIN_LEN = 512
D_FF = 5632
EPS = 1e-6

LANES = 128
VMEM_LIMIT = 56 * 1024 * 1024
NEG = -1e30
SCALE = 1.0 / math.sqrt(HEAD_DIM)
LOG2E = math.log2(math.e)
LN2 = math.log(2.0)
SCALE2 = SCALE * LOG2E
NT = (((1,), (1,)), ((), ()))
DIL_GROUP_BLOCKS = 8
STAGE_STRIDE = 4


def _params(*sem):
    return pltpu.CompilerParams(dimension_semantics=sem, vmem_limit_bytes=VMEM_LIMIT)


def _masked_softmax(s, valid):
    s = jnp.where(valid, s, NEG)
    m = jnp.max(s, axis=-1, keepdims=True)
    e = jnp.where(valid, jnp.exp(s - m), 0.0)
    den = jnp.maximum(jnp.sum(e, axis=-1, keepdims=True), 1e-30)
    return e * (1.0 / den)


def _rms(x, g):
    return x * lax.rsqrt(jnp.mean(x * x, axis=-1, keepdims=True) + EPS) * g


def _iota(shape, dim):
    return lax.broadcasted_iota(jnp.int32, shape, dim)


def _rmsnorm_kernel(x_ref, g_ref, o_ref):
    o_ref[...] = _rms(x_ref[...], g_ref[...]).astype(o_ref.dtype)


def rmsnorm_rows(x, g, tm=512):
    M, D = x.shape
    tm = min(tm, M)
    return pl.pallas_call(
        _rmsnorm_kernel,
        grid=(M // tm,),
        in_specs=[pl.BlockSpec((tm, D), lambda i: (i, 0)),
                  pl.BlockSpec((1, D), lambda i: (0, 0))],
        out_specs=pl.BlockSpec((tm, D), lambda i: (i, 0)),
        out_shape=jax.ShapeDtypeStruct((M, D), BF),
        compiler_params=_params("arbitrary"),
        name="rmsnorm",
    )(x, g.reshape(1, D))


def _mem_kv_kernel(x_ref, g_ref, w_ref, o_ref):
    xn = _rms(x_ref[...], g_ref[...]).astype(BF)
    o_ref[...] = jnp.dot(xn, w_ref[...].astype(BF), preferred_element_type=F32).astype(o_ref.dtype)


def memory_kv_all_layers(mem2, g, w, tn=512):
    Mm, D = mem2.shape
    L, _, N = w.shape
    return pl.pallas_call(
        _mem_kv_kernel,
        grid=(L, N // tn),
        in_specs=[pl.BlockSpec((Mm, D), lambda l, j: (0, 0)),
                  pl.BlockSpec((None, 1, D), lambda l, j: (l, 0, 0)),
                  pl.BlockSpec((None, D, tn), lambda l, j: (l, 0, j))],
        out_specs=pl.BlockSpec((None, Mm, tn), lambda l, j: (l, 0, j)),
        out_shape=jax.ShapeDtypeStruct((L, Mm, N), BF),
        compiler_params=_params("arbitrary", "arbitrary"),
        name="mem_kv",
    )(mem2, g.reshape(L, 1, D), w)


def _mm_kernel(has_scale, has_cast, *refs):
    refs = list(refs)
    x_ref, w_ref = refs.pop(0), refs.pop(0)
    s_ref = refs.pop(0) if has_scale else None
    cast_ref = refs.pop(0) if has_cast else None
    o_ref = refs.pop(0)
    cast_o_ref = refs.pop(0) if has_cast else None
    wbf_ref = refs.pop(0)
    j = pl.program_id(1)

    @pl.when(pl.program_id(0) == 0)
    def _():
        wbf_ref[j] = w_ref[...].astype(BF)

    acc = jnp.dot(x_ref[...], wbf_ref[j], preferred_element_type=F32)
    if has_scale:
        acc = acc * s_ref[...]
    o_ref[...] = acc.astype(o_ref.dtype)
    if has_cast:
        cast_o_ref[...] = cast_ref[...].astype(cast_o_ref.dtype)


def matmul(x, w, layer, *, col0=0, ncols=None, tm=1024, out_dtype=BF, col_scale=None, cast=None,
           name="matmul"):
    M, K = x.shape
    ncols = w.shape[2] - col0 if ncols is None else ncols
    tn = 1024 if (ncols % 1024 == 0 and col0 % 1024 == 0) else 512
    tm = min(tm, M)
    assert col0 % tn == 0 and ncols % tn == 0 and M % tm == 0
    c0 = col0 // tn
    nj = ncols // tn
    in_specs = [pl.BlockSpec((tm, K), lambda i, j: (i, 0)),
                pl.BlockSpec((None, K, tn), lambda i, j: (layer, 0, jnp.where(i == 0, j, 0) + c0))]
    args = [x, w]
    out_specs = [pl.BlockSpec((tm, tn), lambda i, j: (i, j))]
    out_shape = [jax.ShapeDtypeStruct((M, ncols), out_dtype)]
    if col_scale is not None:
        in_specs.append(pl.BlockSpec((1, tn), lambda i, j: (0, j)))
        args.append(col_scale.reshape(1, ncols))
    if cast is not None:
        cast_in, cast_out, cast_shape = _cast_rider(cast[0], cast[1], (M // tm) * nj, lambda i, j: i * nj + j)
        in_specs.append(cast_in)
        args.append(cast[0])
        out_specs.append(cast_out)
        out_shape.append(cast_shape)
    res = pl.pallas_call(
        functools.partial(_mm_kernel, col_scale is not None, cast is not None),
        grid=(M // tm, nj),
        in_specs=in_specs,
        out_specs=out_specs,
        out_shape=out_shape,
        scratch_shapes=[pltpu.VMEM((nj, K, tn), BF)],
        compiler_params=_params("arbitrary", "arbitrary"),
        name=name,
    )(*args)
    return res if cast is not None else res[0]


def _gu_kernel(x_ref, wg_ref, wu_ref, cast_ref, o_ref, cast_o_ref, wgb_ref, wub_ref):
    @pl.when(pl.program_id(1) == 0)
    def _():
        wgb_ref[...] = wg_ref[...].astype(BF)
        wub_ref[...] = wu_ref[...].astype(BF)

    x = x_ref[...]
    g = jnp.dot(x, wgb_ref[...], preferred_element_type=F32)
    u = jnp.dot(x, wub_ref[...], preferred_element_type=F32)
    o_ref[...] = (g * (1.0 / (1.0 + jnp.exp(-g))) * u).astype(o_ref.dtype)
    cast_o_ref[...] = cast_ref[...].astype(cast_o_ref.dtype)


def _cast_rider(src, layer, n_steps, step_of):
    R, C = src.shape[1], src.shape[2]
    rows = R // n_steps
    assert R % n_steps == 0 and rows % 16 == 0
    return (pl.BlockSpec((None, rows, C), lambda *ids: (layer, step_of(*ids), 0)),
            pl.BlockSpec((rows, C), lambda *ids: (step_of(*ids), 0)),
            jax.ShapeDtypeStruct((R, C), BF))


def ffn_gate_up(x, w, layer, w_down, tn=512, tm=1024):
    M, K = x.shape
    nj, ni = D_FF // tn, M // tm
    cast_in, cast_out, cast_shape = _cast_rider(w_down, layer, nj * ni, lambda j, i: j * ni + i)
    return pl.pallas_call(
        _gu_kernel,
        grid=(nj, ni),
        in_specs=[pl.BlockSpec((tm, K), lambda j, i: (i, 0)),
                  pl.BlockSpec((None, K, tn), lambda j, i: (layer, 0, j)),
                  pl.BlockSpec((None, K, tn), lambda j, i: (layer, 0, j + nj)),
                  cast_in],
        out_specs=[pl.BlockSpec((tm, tn), lambda j, i: (i, j)), cast_out],
        out_shape=[jax.ShapeDtypeStruct((M, D_FF), BF), cast_shape],
        scratch_shapes=[pltpu.VMEM((K, tn), BF), pltpu.VMEM((K, tn), BF)],
        compiler_params=_params("arbitrary", "arbitrary"),
        name="ffn_gate_up",
    )(x, w, w, w_down)


def _proj_res_kernel(has_x2, has_next, n_sub, *refs):
    refs = list(refs)
    x1_ref = refs.pop(0)
    x2_ref = refs.pop(0) if has_x2 else None
    w_ref, xres_ref, g_ref = refs.pop(0), refs.pop(0), refs.pop(0)
    gn_ref = refs.pop(0) if has_next else None
    xo_ref = refs.pop(0)
    uo_ref = refs.pop(0) if has_next else None
    k1 = x1_ref.shape[1]
    sub = xo_ref.shape[0] // n_sub
    for h in range(n_sub):
        rows = slice(h * sub, (h + 1) * sub)
        acc = jnp.dot(x1_ref[rows, :], w_ref[:k1, :], preferred_element_type=F32)
        if has_x2:
            acc = acc + jnp.dot(x2_ref[rows, :], w_ref[k1:, :], preferred_element_type=F32)
        xn = xres_ref[rows, :] + _rms(acc, g_ref[...])
        xo_ref[rows, :] = xn
        if has_next:
            uo_ref[rows, :] = _rms(xn, gn_ref[...]).astype(uo_ref.dtype)


def proj_residual_norm(x1, x2, x2_col0, k2, w, layer, xres, g, gn, tm=512, n_sub=2):
    M, K1 = x1.shape
    K, N = w.shape[1], w.shape[2]
    has_x2, has_next = x2 is not None, gn is not None
    assert K == K1 + (k2 if has_x2 else 0)
    row = lambda i: (i, 0)
    in_specs = [pl.BlockSpec((tm, K1), row)]
    args = [x1]
    if has_x2:
        assert x2_col0 % k2 == 0
        in_specs.append(pl.BlockSpec((tm, k2), lambda i: (i, x2_col0 // k2)))
        args.append(x2)
    in_specs += [pl.BlockSpec((None, K, N), lambda i: (layer, 0, 0), pipeline_mode=pl.Buffered(1)),
                 pl.BlockSpec((tm, N), row),
                 pl.BlockSpec((1, N), lambda i: (0, 0))]
    args += [w, xres, g.reshape(1, N)]
    out_specs = [pl.BlockSpec((tm, N), row)]
    out_shape = [jax.ShapeDtypeStruct((M, N), F32)]
    if has_next:
        in_specs.append(pl.BlockSpec((1, N), lambda i: (0, 0)))
        args.append(gn.reshape(1, N))
        out_specs.append(pl.BlockSpec((tm, N), row))
        out_shape.append(jax.ShapeDtypeStruct((M, N), BF))
    res = pl.pallas_call(
        functools.partial(_proj_res_kernel, has_x2, has_next, n_sub),
        grid=(M // tm,),
        in_specs=in_specs,
        out_specs=out_specs,
        out_shape=out_shape,
        compiler_params=_params("arbitrary"),
        name="proj_residual_norm",
    )(*args)
    return (res[0], res[1]) if has_next else (res[0], None)


def _divmod_pow2(n, d):
    assert d & (d - 1) == 0
    return lax.shift_right_logical(n, d.bit_length() - 1), n & (d - 1)


def _dil_kernel(*refs):
    ng = len(DIL_GROUPS)
    q_refs, k_refs, v_refs = refs[:ng], refs[ng:2 * ng], refs[2 * ng:3 * ng]
    o_ref, o_scr, l_scr, stage_scr, ostage_scr = refs[3 * ng:]
    S = o_ref.shape[1]
    blk = A_BLOCK
    n_heads = ng * A_HEADS_PER_GROUP
    j = pl.program_id(1)
    for gi, (w, d) in enumerate(DIL_GROUPS):
        q_ref, k_ref, v_ref = q_refs[gi], k_refs[gi], v_refs[gi]
        nqb = S // d // blk
        max_dist = w // d
        has_prev = nqb > 1
        nk = 2 * blk if has_prev else blk
        koff = blk if has_prev else 0
        head = (j + (gi * A_HEADS_PER_GROUP + 1)).astype(F32)
        slope = jnp.exp2(jnp.zeros((1, 1), F32) + head * (-8.0 / n_heads))
        kidx = _iota((blk, nk), 1)
        dist = _iota((blk, nk), 0) + koff - kidx
        band = (dist >= 0) & (dist <= max_dist)
        bias = jnp.where(band, (kidx - koff).astype(F32) * (slope * (d * LOG2E)), NEG)
        bias_first = jnp.where(kidx >= koff, bias, NEG)
        row_term = _iota((blk, 1), 0).astype(F32) * (slope * (-float(d)))

        staged = d > STAGE_STRIDE
        if staged:
            assert d % STAGE_STRIDE == 0 and d // STAGE_STRIDE <= STAGE_STRIDE
            quarter = S // STAGE_STRIDE
            for t_idx, ref in enumerate((q_ref, k_ref, v_ref)):
                for r1 in range(STAGE_STRIDE):
                    stage_scr[t_idx, r1 * quarter:(r1 + 1) * quarter, :] = ref[0, pl.ds(r1, quarter, stride=STAGE_STRIDE), :]

        def out_rows(r, i, d=d):
            start = r + i * (d * blk)
            return pl.ds(start, blk, stride=d) if d > 1 else pl.ds(start, blk)

        def staged_rows(r, i, d=d):
            s2 = d // STAGE_STRIDE
            r2, r1 = _divmod_pow2(r, STAGE_STRIDE)
            return pl.ds(r1 * (S // STAGE_STRIDE) + r2 + i * (s2 * blk), blk, stride=s2)

        def load(t_idx, ref, r, i, staged=staged):
            return stage_scr[t_idx, staged_rows(r, i), :] if staged else ref[0, out_rows(r, i), :]

        def body(n0, carry, gi=gi, d=d, has_prev=has_prev, q_ref=q_ref, k_ref=k_ref, v_ref=v_ref,
                 bias=bias, bias_first=bias_first, row_term=row_term, out_rows=out_rows, load=load,
                 staged=staged, staged_rows=staged_rows):
            ri = [_divmod_pow2(n0 * DIL_GROUP_BLOCKS + g, d)[::-1] for g in range(DIL_GROUP_BLOCKS)]
            qs = [load(0, q_ref, r, i).astype(BF) for r, i in ri]
            if has_prev:
                ips = [jnp.maximum(i - 1, 0) for _, i in ri]
                ks = [jnp.concatenate([load(1, k_ref, r, ip), load(1, k_ref, r, i)], axis=0).astype(BF)
                      for (r, i), ip in zip(ri, ips)]
                vs = [jnp.concatenate([load(2, v_ref, r, ip), load(2, v_ref, r, i)], axis=0).astype(BF)
                      for (r, i), ip in zip(ri, ips)]
                bs = [jnp.where(i > 0, bias, bias_first) for _, i in ri]
            else:
                ks = [load(1, k_ref, r, i).astype(BF) for r, i in ri]
                vs = [load(2, v_ref, r, i).astype(BF) for r, i in ri]
                bs = [bias] * len(ri)
            ss = [lax.dot_general(q, k, NT, preferred_element_type=F32) + b for q, k, b in zip(qs, ks, bs)]
            ms = [jnp.max(s, axis=-1, keepdims=True) for s in ss]
            ps = [jnp.exp2(s - m) for s, m in zip(ss, ms)]
            ls = [jnp.sum(p, axis=-1, keepdims=True) for p in ps]
            os_ = [jnp.dot(p.astype(BF), v, preferred_element_type=F32) * (1.0 / l) for p, v, l in zip(ps, vs, ls)]
            for (r, i), o, m, l in zip(ri, os_, ms, ls):
                lse_b = jnp.broadcast_to((m + jnp.log2(l)) * LN2 + row_term, (blk, HEAD_DIM))
                if staged:
                    rows = staged_rows(r, i)
                    ostage_scr[0, rows, :] = o
                    ostage_scr[1, rows, :] = lse_b
                else:
                    rows = out_rows(r, i)
                    o_scr[gi, rows, :] = o
                    l_scr[gi, rows, :] = lse_b
            return carry

        assert (d * nqb) % DIL_GROUP_BLOCKS == 0
        lax.fori_loop(0, d * nqb // DIL_GROUP_BLOCKS, body, 0)
        if staged:
            for r1 in range(STAGE_STRIDE):
                dst = pl.ds(r1, quarter, stride=STAGE_STRIDE)
                o_scr[gi, dst, :] = ostage_scr[0, r1 * quarter:(r1 + 1) * quarter, :]
                l_scr[gi, dst, :] = ostage_scr[1, r1 * quarter:(r1 + 1) * quarter, :]

    tr = 256
    for c in range(S // tr):
        rows = slice(c * tr, (c + 1) * tr)
        ls = [l_scr[gi, rows, :] for gi in range(ng)]
        m = functools.reduce(jnp.maximum, ls)
        es = [jnp.exp(x - m) for x in ls]
        num = sum(e * o_scr[gi, rows, :] for gi, e in enumerate(es))
        o_ref[0, rows, :] = (num * (1.0 / sum(es))).astype(o_ref.dtype)


def dilated_attention(q, k, v):
    B, S, _ = q.shape
    ng, Hg = len(DIL_GROUPS), A_HEADS_PER_GROUP
    for w, d in DIL_GROUPS:
        assert w // d <= A_BLOCK and S % (d * A_BLOCK) == 0
    specs = [pl.BlockSpec((1, S, HEAD_DIM), lambda b, j, gi=gi: (b, 0, gi * Hg + j)) for gi in range(ng)]
    return pl.pallas_call(
        _dil_kernel,
        grid=(B, Hg),
        in_specs=specs * 3,
        out_specs=pl.BlockSpec((1, S, HEAD_DIM), lambda b, j: (b, 0, j)),
        out_shape=jax.ShapeDtypeStruct((B, S, Hg * HEAD_DIM), BF),
        scratch_shapes=[pltpu.VMEM((ng, S, HEAD_DIM), F32), pltpu.VMEM((ng, S, HEAD_DIM), F32),
                        pltpu.VMEM((3, S, HEAD_DIM), F32), pltpu.VMEM((2, S, HEAD_DIM), F32)],
        compiler_params=_params("arbitrary", "arbitrary"),
        name="dilated_attention",
    )(*([q] * ng + [k] * ng + [v] * ng))


def _mem_attn_kernel(q_ref, kv_ref, o_ref):
    for h in range(MEM_HEADS):
        c = slice(h * HEAD_DIM, (h + 1) * HEAD_DIM)
        k = kv_ref[0, :, c]
        v = kv_ref[0, :, MEM_HEADS * HEAD_DIM + h * HEAD_DIM:MEM_HEADS * HEAD_DIM + (h + 1) * HEAD_DIM]
        s = lax.dot_general(q_ref[0, :, c], k, NT, preferred_element_type=F32) * SCALE
        m = jnp.max(s, axis=-1, keepdims=True)
        e = jnp.exp(s - m)
        p = e * (1.0 / jnp.sum(e, axis=-1, keepdims=True))
        o_ref[0, :, c] = jnp.dot(p.astype(BF), v, preferred_element_type=F32).astype(o_ref.dtype)


def memory_attention(q, q_col0, kv, layer, tq=512):
    B, S, _ = q.shape
    w = MEM_HEADS * HEAD_DIM
    cb = q_col0 // w
    return pl.pallas_call(
        _mem_attn_kernel,
        grid=(B, S // tq),
        in_specs=[pl.BlockSpec((1, tq, w), lambda b, i: (b, i, cb)),
                  pl.BlockSpec((None, 1, N_MEM, 2 * w), lambda b, i: (layer, b, 0, 0))],
        out_specs=pl.BlockSpec((1, tq, w), lambda b, i: (b, i, 0)),
        out_shape=jax.ShapeDtypeStruct((B, S, w), BF),
        compiler_params=_params("arbitrary", "arbitrary"),
        name="memory_attention",
    )(q, kv)


def _gelu_tanh(x):
    return 0.5 * x * (1.0 + jnp.tanh(math.sqrt(2.0 / math.pi) * (x + 0.044715 * (x * x * x))))


def _compress_kernel(out_scale, x_ref, w1_ref, w2_ref, pe_ref, o_ref, w1b_ref):
    @pl.when((pl.program_id(0) == 0) & (pl.program_id(1) == 0))
    def _():
        w1b_ref[...] = w1_ref[...].astype(BF)

    nchunk = o_ref.shape[1]
    top = jnp.zeros((nchunk, CMP_HIDDEN), F32)
    bot = jnp.zeros((nchunk, CMP_HIDDEN), F32)
    for l in range(CMP_STRIDE):
        xl = x_ref[0, pl.ds(l, nchunk, stride=CMP_STRIDE), :]
        lo, hi = l, CMP_STRIDE + l
        top += jnp.dot((xl + pe_ref[lo:lo + 1, :]).astype(BF),
                       w1b_ref[lo * HEAD_DIM:(lo + 1) * HEAD_DIM, :], preferred_element_type=F32)
        bot += jnp.dot((xl + pe_ref[hi:hi + 1, :]).astype(BF),
                       w1b_ref[hi * HEAD_DIM:(hi + 1) * HEAD_DIM, :], preferred_element_type=F32)
    hid = _gelu_tanh(top + pltpu.roll(bot, nchunk - 1, 0))
    out = jnp.dot(hid.astype(BF), w2_ref[...].astype(BF), preferred_element_type=F32)
    o_ref[0] = (out * out_scale).astype(o_ref.dtype)


def compress_blocks(kvc, sec, w1, w2, pe, out_scale=1.0):
    B, S, C = kvc.shape
    assert CMP_LEN == 2 * CMP_STRIDE
    G = B_KV_GROUPS
    nchunk = S // CMP_STRIDE
    return pl.pallas_call(
        functools.partial(_compress_kernel, out_scale),
        grid=(B, G),
        in_specs=[pl.BlockSpec((1, S, HEAD_DIM), lambda b, g: (b, 0, sec * G + g)),
                  pl.BlockSpec(w1.shape, lambda b, g: (0, 0)),
                  pl.BlockSpec(w2.shape, lambda b, g: (0, 0)),
                  pl.BlockSpec(pe.shape, lambda b, g: (0, 0))],
        out_specs=pl.BlockSpec((1, nchunk, HEAD_DIM), lambda b, g: (b, 0, g)),
        out_shape=jax.ShapeDtypeStruct((B, nchunk, G * HEAD_DIM), BF),
        scratch_shapes=[pltpu.VMEM(w1.shape, BF)],
        compiler_params=_params("arbitrary", "arbitrary"),
        name=f"compress_{sec}",
    )(kvc, w1, w2, pe)


ONES_ROWS = 16
AUG_K = LANES
NEG_ROWS = 16


def _key_aug(nk, bps):
    idx = _iota((nk, AUG_K), 0)
    lane = _iota((nk, AUG_K), 1)
    piece = jnp.where((lane - NEG_ROWS) % 2 == 0, (idx // 256) * 256, idx % 256)
    val = jnp.where(lane < bps, (idx // SLC_LEN == lane).astype(jnp.int32),
                    jnp.where((lane >= NEG_ROWS) & (lane < NEG_ROWS + 6), piece, 0))
    return val.astype(F32).astype(BF)


def _offset_masks(n, nk, tq, lo, hi):
    dist = (_iota((n, nk, tq), 0) * tq + _iota((n, nk, tq), 2)) - _iota((n, nk, tq), 1)
    return jnp.where((dist >= lo) & (dist <= hi), 0.0, NEG).astype(F32)


def _t_bf(x):
    return x.astype(F32).T.astype(BF)


def _nsa_kernel(tq, q_ref, gate_ref, kc_ref, vc_ref, ks_ref, vs_ref, kw_ref, vw_ref,
                sl_ref, kaug_w_ref, kaug_s_ref, wmask_ref, cmask_ref, o_ref,
                vst_ref, vwt_ref, qt_ref, ocmp_ref, neg_ref, raw_a_ref, raw_b_ref, pc_ref):
    S = ks_ref.shape[1]
    ncmp = kc_ref.shape[1]
    nsel = S // SLC_LEN
    i = pl.program_id(2)

    t0 = i * tq
    tw = B_REP * tq
    nt = S // tq
    n_back = -(-(WIN_LEN - 1) // tq)
    nwin = (n_back + 1) * tq
    kcs = vst_ref.shape[2]
    bps = kcs // SLC_LEN

    def per_head(fn, n=B_REP):
        return jnp.concatenate([fn(r) for r in range(n)], axis=1)

    def bf_exact(x):
        return x.astype(BF).astype(F32)

    @pl.when(i == 0)
    def _():
        for src_ref, dst_ref in ((vs_ref, vst_ref), (vw_ref, vwt_ref)):
            n, _, w = dst_ref.shape
            ones = jnp.ones((ONES_ROWS, w), BF)
            for c in range(n):
                dst_ref[c] = jnp.concatenate([_t_bf(src_ref[0, c * w:(c + 1) * w, :]), ones], axis=0)
        vct = _t_bf(vc_ref[0])
        for t in range(nt):
            for r in range(B_REP):
                qt_ref[t, r] = _t_bf(q_ref[0, t * tq:(t + 1) * tq, r * HEAD_DIM:(r + 1) * HEAD_DIM])
        q_w = jnp.concatenate([qt_ref[t, r] for r in range(B_REP) for t in range(nt)], axis=1)
        sl2_w = per_head(lambda r: jnp.broadcast_to(sl_ref[0, 0:1, r:r + 1], (1, S))) * LOG2E
        pos = _iota((1, S), 1)
        pos_w = per_head(lambda r: pos)

        dist_c = pos_w - (_iota((ncmp, 1), 0) * CMP_STRIDE + (CMP_LEN - 1))
        s = (jnp.dot(kc_ref[0], q_w, preferred_element_type=F32)
             + jnp.where(dist_c >= 0, dist_c.astype(F32) * (-sl2_w), NEG))
        e = jnp.exp2(s - jnp.max(s, axis=0, keepdims=True))
        has_block = (pos_w >= CMP_LEN - 1).astype(F32)
        p = e * (has_block / jnp.maximum(jnp.sum(e, axis=0, keepdims=True), 1e-30))
        o_cmp_w = jnp.dot(vct, p.astype(BF), preferred_element_type=F32)
        for t in range(nt):
            for r in range(B_REP):
                ocmp_ref[t, r] = o_cmp_w[:, r * S + t * tq:r * S + (t + 1) * tq]

        ratio = SLC_LEN // CMP_STRIDE
        assert SLC_LEN % CMP_STRIDE == 0 and CMP_LEN <= SLC_LEN and ncmp >= nsel * ratio
        p_heads = sum(p[:, r * S:(r + 1) * S] for r in range(B_REP))
        for lt in range(S // LANES):
            pc_ref[lt] = p_heads[:, lt * LANES:(lt + 1) * LANES]
        imp = jnp.zeros((nsel, S), F32)
        for o in range(1 - (-(-CMP_LEN // CMP_STRIDE)), ratio):
            overlap = max(min(o * CMP_STRIDE + CMP_LEN, SLC_LEN) - max(o * CMP_STRIDE, 0), 0)
            if overlap == 0:
                continue
            part = jnp.concatenate([pc_ref[lt, pl.ds(o % ratio, nsel, stride=ratio), :]
                                    for lt in range(S // LANES)], axis=1)
            if o < 0:
                part = jnp.where(_iota((nsel, S), 0) == 0, 0.0, pltpu.roll(part, 1, 0))
            imp = imp + (overlap / CMP_LEN) * part

        jb = _iota((nsel, S), 0)
        cur = pos // SLC_LEN
        forced = (jb == 0) | (jb == cur) | (jb == cur - 1)
        imp = jnp.where(forced, SEL_FORCE, jnp.where(jb > cur, -SEL_FORCE, imp))
        beaten_by = jnp.zeros((nsel, S), jnp.int32)
        for j in range(nsel):
            row = imp[j:j + 1, :]
            beats = (row > imp) | ((row == imp) & (jb > j))
            beaten_by = beaten_by + beats.astype(jnp.int32)
        neg_w = jnp.where(beaten_by >= min(SLC_TOPK, nsel), NEG, 0.0)
        for t in range(nt):
            neg_ref[t] = neg_w[:, t * tq:(t + 1) * tq]

    q_all = per_head(lambda r: qt_ref[i, r])
    o_cmp = per_head(lambda r: ocmp_ref[i, r])
    sl_row = per_head(lambda r: jnp.broadcast_to(sl_ref[0, 0:1, r:r + 1], (1, tq)))
    sl2_row = sl_row * LOG2E

    c1 = bf_exact(sl2_row)
    c2 = bf_exact(sl2_row - c1)
    c3 = bf_exact(sl2_row - c1 - c2)
    slope_rows = jnp.concatenate([c1, c1, c2, c2, c3, c3, jnp.zeros((AUG_K - NEG_ROWS - 6, tw), F32)],
                                 axis=0).astype(BF)

    def scores(k_c, k_aug, neg_rows):
        q_full = jnp.concatenate([q_all, neg_rows, slope_rows], axis=0)
        return jnp.dot(jnp.concatenate([k_c, k_aug], axis=1), q_full, preferred_element_type=F32)

    def neg_rows_of(n):
        rows = neg_ref[i, pl.ds(pl.multiple_of(n * bps, bps), bps), :]
        return jnp.concatenate([per_head(lambda r: rows), jnp.zeros((NEG_ROWS - bps, tw), F32)], axis=0).astype(BF)

    no_rows = jnp.zeros((NEG_ROWS, tw), BF)
    c0 = jnp.maximum(i - n_back, 0)
    wbase = pl.multiple_of(c0 * tq, tq)
    raw_w = scores(kw_ref[0, pl.ds(wbase, nwin), :], kaug_w_ref[...], no_rows)
    k_aug = kaug_s_ref[...]
    n_last = (t0 + tq - 1) // kcs
    raw_d = scores(ks_ref[0, pl.ds(pl.multiple_of(n_last * kcs, kcs), kcs), :], k_aug, no_rows)

    def softmax_step(raw, vt_c, mask, c_off, st):
        m, acc = st
        s = raw if mask is None else raw + per_head(lambda r: mask)
        m_new = jnp.maximum(m, jnp.max(s, axis=0, keepdims=True) + c_off)
        p = jnp.exp2(s - (m_new - c_off))
        acc = jnp.exp2(m - m_new) * acc + jnp.dot(vt_c, p.astype(BF), preferred_element_type=F32)
        return m_new, acc

    def finish(st):
        _, acc = st
        return acc[:HEAD_DIM] * (1.0 / acc[HEAD_DIM:HEAD_DIM + 1])

    init = (jnp.full((1, tw), NEG, F32), jnp.zeros((HEAD_DIM + ONES_ROWS, tw), F32))
    no_off = jnp.zeros((1, 1), F32)

    vwt = jnp.concatenate([vwt_ref[c0 + j] for j in range(n_back + 1)], axis=1)
    o_win = finish(softmax_step(raw_w, vwt, wmask_ref[i - c0], no_off, init))

    def slab_scores(n):
        return scores(ks_ref[0, pl.ds(pl.multiple_of(n * kcs, kcs), kcs), :], k_aug, neg_rows_of(n))

    raw_a_ref[...] = slab_scores(0)
    sel_rows = [jnp.broadcast_to(neg_ref[i, pl.ds(n_last * bps + b, 1), :], (SLC_LEN, tq)) for b in range(bps)]
    mask_d = jnp.concatenate(sel_rows, axis=0) + cmask_ref[i - n_last * (kcs // tq)]
    st = softmax_step(raw_d, vst_ref[n_last], mask_d, (n_last * kcs - t0).astype(F32) * sl2_row, init)

    def sel_slab(n, st, src_ref, dst_ref):
        dst_ref[...] = slab_scores(jnp.minimum(n + 1, n_last))
        return softmax_step(src_ref[...], vst_ref[n], None, (n * kcs - t0).astype(F32) * sl2_row, st)

    def sel_pair(j, st):
        st = sel_slab(2 * j, st, raw_a_ref, raw_b_ref)
        return lax.cond(2 * j + 1 < n_last, lambda s: sel_slab(2 * j + 1, s, raw_b_ref, raw_a_ref),
                        lambda s: s, st)

    o_slc = finish(lax.fori_loop(0, (n_last + 1) // 2, sel_pair, st))

    gate_t = (1.0 / (1.0 + jnp.exp(-gate_ref[0]))).T
    for r in range(B_REP):
        c = slice(r * tq, (r + 1) * tq)
        o_t = (gate_t[3 * r:3 * r + 1] * o_cmp[:, c] + gate_t[3 * r + 1:3 * r + 2] * o_slc[:, c]
               + gate_t[3 * r + 2:3 * r + 3] * o_win[:, c])
        o_ref[0, :, r * HEAD_DIM:(r + 1) * HEAD_DIM] = o_t.T.astype(o_ref.dtype)


def nsa_attention(qp, gates, kc, vc, kv, tq=256):
    B, S, _ = qp.shape
    G = B_KV_GROUPS
    gq = B_REP * HEAD_DIM
    ncmp = kc.shape[1]
    kcs = 2 * tq
    nsel = S // SLC_LEN
    assert kcs % SLC_LEN == 0 and S % kcs == 0 and S >= (-(-(WIN_LEN - 1) // tq) + 1) * tq
    slopes = 2.0 ** (-8.0 * jnp.arange(1, B_HEADS + 1, dtype=F32) / B_HEADS)
    sl = jnp.zeros((G, 8, LANES), F32).at[:, :, :B_REP].set(
        jnp.broadcast_to(slopes.reshape(G, 1, B_REP), (G, 8, B_REP)))
    kvspec = lambda sec: pl.BlockSpec((1, S, HEAD_DIM), lambda b, g, i: (b, 0, sec * G + g))
    cspec = pl.BlockSpec((1, ncmp, HEAD_DIM), lambda b, g, i: (b, 0, g))
    vt_rows = HEAD_DIM + ONES_ROWS
    n_back = -(-(WIN_LEN - 1) // tq)
    nwin = (n_back + 1) * tq
    bps = kcs // SLC_LEN
    assert bps <= NEG_ROWS and NEG_ROWS + 6 <= AUG_K
    consts = [_key_aug(nwin, bps), _key_aug(kcs, bps), _offset_masks(n_back + 1, nwin, tq, 0, WIN_LEN - 1),
              _offset_masks(kcs // tq, kcs, tq, 0, S)]
    whole = lambda a: pl.BlockSpec(a.shape, lambda b, g, i, nd=a.ndim: (0,) * nd)
    return pl.pallas_call(
        functools.partial(_nsa_kernel, tq),
        grid=(B, G, S // tq),
        in_specs=[pl.BlockSpec((1, S, gq), lambda b, g, i: (b, 0, g)),
                  pl.BlockSpec((1, tq, LANES), lambda b, g, i: (b, i, g)),
                  cspec, cspec, kvspec(0), kvspec(1), kvspec(2), kvspec(3),
                  pl.BlockSpec((1, 8, LANES), lambda b, g, i: (g, 0, 0))] + [whole(a) for a in consts],
        out_specs=pl.BlockSpec((1, tq, gq), lambda b, g, i: (b, i, g)),
        out_shape=jax.ShapeDtypeStruct((B, S, B_HEADS * HEAD_DIM), BF),
        scratch_shapes=[pltpu.VMEM((S // kcs, vt_rows, kcs), BF), pltpu.VMEM((S // tq, vt_rows, tq), BF),
                        pltpu.VMEM((S // tq, B_REP, HEAD_DIM, tq), BF), pltpu.VMEM((S // tq, B_REP, HEAD_DIM, tq), F32),
                        pltpu.VMEM((S // tq, nsel, tq), F32),
                        pltpu.VMEM((kcs, B_REP * tq), F32), pltpu.VMEM((kcs, B_REP * tq), F32),
                        pltpu.VMEM((S // LANES, ncmp, LANES), F32)],
        compiler_params=_params("arbitrary", "arbitrary", "arbitrary"),
        name="nsa_attention",
    )(qp, gates, kc, vc, kv, kv, kv, kv, sl, *consts)


def kernel(x, mem, norm_g, a_w_in, a_w_out, b_w_in, b_w_out, mem_w_kv, ffn_w_gu, ffn_w_down,
           kv_norm_g, kv_w, cmp_pe, cmp_wk1, cmp_wk2, cmp_wv1, cmp_wv2):
    B, S, D = x.shape
    M = B * S
    n_a = DEPTH // 2
    qa_cols = len(DIL_GROUPS) * A_HEADS_PER_GROUP * HEAD_DIM
    qb_cols = B_HEADS * HEAD_DIM
    mq_cols = MEM_HEADS * HEAD_DIM
    n_gate = 3 * B_HEADS
    G = B_KV_GROUPS

    b_w_mq = b_w_in[:, :, qb_cols + n_gate:]
    b_w_gate = jnp.pad(b_w_in[:, :, qb_cols:qb_cols + n_gate].reshape(-1, D, G, n_gate // G),
                       ((0, 0), (0, 0), (0, 0), (0, LANES - n_gate // G))).reshape(-1, D, G * LANES)
    kv_w3 = kv_w[None]

    x2 = x.reshape(M, D)
    mkv = memory_kv_all_layers(mem.reshape(B * N_MEM, D), norm_g[:, 4], mem_w_kv).reshape(
        DEPTH, B, N_MEM, 2 * mq_cols)
    u = rmsnorm_rows(x2, norm_g[0, 0])
    shared = None
    for l in range(DEPTH):
        g = norm_g[l]
        if l < n_a:
            q, w_out = matmul(u, a_w_in, l, col0=0, ncols=qa_cols, out_dtype=F32, cast=(a_w_out, l), name="a_q")
            k = matmul(u, a_w_in, l, col0=qa_cols, ncols=qa_cols, out_dtype=F32,
                       col_scale=jnp.full((qa_cols,), SCALE2, F32), name="a_k")
            v = matmul(u, a_w_in, l, col0=2 * qa_cols, ncols=qa_cols, out_dtype=F32, name="a_v")
            mq = matmul(u, a_w_in, l, col0=3 * qa_cols, ncols=mq_cols, name="a_mq").reshape(B, S, mq_cols)
            o_main = dilated_attention(*(t.reshape(B, S, qa_cols) for t in (q, k, v)))
            o_mem = memory_attention(mq, 0, mkv, l)
        else:
            li = l - n_a
            if shared is None:
                kvn = rmsnorm_rows(x2, kv_norm_g)
                kvc = matmul(kvn, kv_w3, 0, col0=0, ncols=2 * G * HEAD_DIM, out_dtype=F32,
                             name="kv_cmp").reshape(B, S, 2 * G * HEAD_DIM)
                sec = G * HEAD_DIM
                key_scale = jnp.tile(jnp.concatenate([jnp.full((sec,), SCALE2, F32), jnp.ones((sec,), F32)]), 2)
                kvr = matmul(kvn, kv_w3, 0, col0=2 * G * HEAD_DIM, ncols=4 * G * HEAD_DIM, col_scale=key_scale,
                             name="kv_rest").reshape(B, S, 4 * G * HEAD_DIM)
                kc = compress_blocks(kvc, 0, cmp_wk1, cmp_wk2, cmp_pe[0], out_scale=SCALE2)
                vc = compress_blocks(kvc, 1, cmp_wv1, cmp_wv2, cmp_pe[1])
                shared = (kc, vc, kvr)
            qb = matmul(u, b_w_in, li, col0=0, ncols=qb_cols, name="b_q").reshape(B, S, qb_cols)
            mq = matmul(u, b_w_mq, li, name="b_mq").reshape(B, S, mq_cols)
            gates, w_out = matmul(u, b_w_gate, li, out_dtype=F32, cast=(b_w_out, li), name="b_gate")
            o_main = nsa_attention(qb, gates.reshape(B, S, G * LANES), *shared)
            o_mem = memory_attention(mq, 0, mkv, l)
        x2, u = proj_residual_norm(o_main.reshape(M, -1), o_mem.reshape(M, mq_cols), 0, mq_cols,
                                   w_out[None], 0, x2, g[1], g[2])
        h, w_down = ffn_gate_up(u, ffn_w_gu, l, ffn_w_down)
        gn = norm_g[l + 1, 0] if l + 1 < DEPTH else None
        x2, u = proj_residual_norm(h, None, 0, 0, w_down[None], 0, x2, g[3], gn, tm=256, n_sub=2)
    return x2.reshape(B, S, D)
```

```python
import functools
import math

import jax
import jax.numpy as jnp
from jax import lax
from jax.experimental import pallas as pl
from jax.experimental.pallas import tpu as pltpu

BF = jnp.bfloat16
F32 = jnp.float32

D_MODEL = 2048
DEPTH = 4
HEAD_DIM = 128
N_MEM = 256
MEM_HEADS = 4
DIL_GROUPS = ((128, 1), (512, 4), (2048, 16))
A_HEADS_PER_GROUP = 8
A_BLOCK = 128
B_HEADS = 12
B_KV_GROUPS = 4
B_REP = B_HEADS // B_KV_GROUPS
CMP_LEN = 32
CMP_STRIDE = 16
CMP_HIDDEN = 512
SLC_LEN = 64
SLC_TOPK = 16
SEL_FORCE = 1e4
WIN_LEN = 512
D_FF = 5632
EPS = 1e-6

LANES = 128
VMEM_LIMIT = 56 * 1024 * 1024
NEG = -1e30
SCALE = 1.0 / math.sqrt(HEAD_DIM)
LOG2E = math.log2(math.e)
LN2 = math.log(2.0)
SCALE2 = SCALE * LOG2E
NT = (((1,), (1,)), ((), ()))
DIL_GROUP_BLOCKS = 16
STAGE_STRIDE = 4


def _params(*sem):
    return pltpu.CompilerParams(dimension_semantics=sem, vmem_limit_bytes=VMEM_LIMIT)


def _masked_softmax(s, valid):
    s = jnp.where(valid, s, NEG)
    m = jnp.max(s, axis=-1, keepdims=True)
    e = jnp.where(valid, jnp.exp(s - m), 0.0)
    den = jnp.maximum(jnp.sum(e, axis=-1, keepdims=True), 1e-30)
    return e * (1.0 / den)


def _rms(x, g):
    return x * lax.rsqrt(jnp.mean(x * x, axis=-1, keepdims=True) + EPS) * g


def _iota(shape, dim):
    return lax.broadcasted_iota(jnp.int32, shape, dim)


def _rmsnorm_kernel(x_ref, g_ref, o_ref):
    o_ref[...] = _rms(x_ref[...], g_ref[...]).astype(o_ref.dtype)


def rmsnorm_rows(x, g, tm=1024):
    M, D = x.shape
    tm = min(tm, M)
    return pl.pallas_call(
        _rmsnorm_kernel,
        grid=(M // tm,),
        in_specs=[pl.BlockSpec((tm, D), lambda i: (i, 0)),
                  pl.BlockSpec((1, D), lambda i: (0, 0))],
        out_specs=pl.BlockSpec((tm, D), lambda i: (i, 0)),
        out_shape=jax.ShapeDtypeStruct((M, D), BF),
        compiler_params=_params("arbitrary"),
        name="rmsnorm",
    )(x, g.reshape(1, D))


def _mem_kv_kernel(x_ref, g_ref, w_ref, o_ref):
    xn = _rms(x_ref[...], g_ref[...]).astype(BF)
    o_ref[...] = jnp.dot(xn, w_ref[...].astype(BF), preferred_element_type=F32).astype(o_ref.dtype)


def memory_kv_all_layers(mem2, g, w, tn=512):
    Mm, D = mem2.shape
    L, _, N = w.shape
    return pl.pallas_call(
        _mem_kv_kernel,
        grid=(L, N // tn),
        in_specs=[pl.BlockSpec((Mm, D), lambda l, j: (0, 0)),
                  pl.BlockSpec((None, 1, D), lambda l, j: (l, 0, 0)),
                  pl.BlockSpec((None, D, tn), lambda l, j: (l, 0, j))],
        out_specs=pl.BlockSpec((None, Mm, tn), lambda l, j: (l, 0, j)),
        out_shape=jax.ShapeDtypeStruct((L, Mm, N), BF),
        compiler_params=_params("arbitrary", "arbitrary"),
        name="mem_kv",
    )(mem2, g.reshape(L, 1, D), w)


def _mm_kernel(has_scale, has_cast, *refs):
    refs = list(refs)
    x_ref, w_ref = refs.pop(0), refs.pop(0)
    s_ref = refs.pop(0) if has_scale else None
    cast_ref = refs.pop(0) if has_cast else None
    o_ref = refs.pop(0)
    cast_o_ref = refs.pop(0) if has_cast else None
    wbf_ref = refs.pop(0)
    j = pl.program_id(1)

    @pl.when(pl.program_id(0) == 0)
    def _():
        wbf_ref[j] = w_ref[...].astype(BF)

    acc = jnp.dot(x_ref[...], wbf_ref[j], preferred_element_type=F32)
    if has_scale:
        acc = acc * s_ref[...]
    o_ref[...] = acc.astype(o_ref.dtype)
    if has_cast:
        cast_o_ref[...] = cast_ref[...].astype(cast_o_ref.dtype)


def matmul(x, w, layer, *, col0=0, ncols=None, tm=1024, out_dtype=BF, col_scale=None, cast=None,
           name="matmul"):
    M, K = x.shape
    ncols = w.shape[2] - col0 if ncols is None else ncols
    tn = 1024 if (ncols % 1024 == 0 and col0 % 1024 == 0) else 512
    tm = min(tm, M)
    assert col0 % tn == 0 and ncols % tn == 0 and M % tm == 0
    c0 = col0 // tn
    nj = ncols // tn
    in_specs = [pl.BlockSpec((tm, K), lambda i, j: (i, 0)),
                pl.BlockSpec((None, K, tn), lambda i, j: (layer, 0, jnp.where(i == 0, j, 0) + c0))]
    args = [x, w]
    out_specs = [pl.BlockSpec((tm, tn), lambda i, j: (i, j))]
    out_shape = [jax.ShapeDtypeStruct((M, ncols), out_dtype)]
    if col_scale is not None:
        in_specs.append(pl.BlockSpec((1, tn), lambda i, j: (0, j)))
        args.append(col_scale.reshape(1, ncols))
    if cast is not None:
        cast_in, cast_out, cast_shape = _cast_rider(cast[0], cast[1], (M // tm) * nj, lambda i, j: i * nj + j)
        in_specs.append(cast_in)
        args.append(cast[0])
        out_specs.append(cast_out)
        out_shape.append(cast_shape)
    res = pl.pallas_call(
        functools.partial(_mm_kernel, col_scale is not None, cast is not None),
        grid=(M // tm, nj),
        in_specs=in_specs,
        out_specs=out_specs,
        out_shape=out_shape,
        scratch_shapes=[pltpu.VMEM((nj, K, tn), BF)],
        compiler_params=_params("arbitrary", "arbitrary"),
        name=name,
    )(*args)
    return res if cast is not None else res[0]


def _gu_kernel(x_ref, wg_ref, wu_ref, cast_ref, o_ref, cast_o_ref, wgb_ref, wub_ref):
    @pl.when(pl.program_id(1) == 0)
    def _():
        wgb_ref[...] = wg_ref[...].astype(BF)
        wub_ref[...] = wu_ref[...].astype(BF)

    x = x_ref[...]
    g = jnp.dot(x, wgb_ref[...], preferred_element_type=F32)
    u = jnp.dot(x, wub_ref[...], preferred_element_type=F32)
    o_ref[...] = (g * (1.0 / (1.0 + jnp.exp(-g))) * u).astype(o_ref.dtype)
    cast_o_ref[...] = cast_ref[...].astype(cast_o_ref.dtype)


def _cast_rider(src, layer, n_steps, step_of):
    R, C = src.shape[1], src.shape[2]
    rows = R // n_steps
    assert R % n_steps == 0 and rows % 16 == 0
    return (pl.BlockSpec((None, rows, C), lambda *ids: (layer, step_of(*ids), 0)),
            pl.BlockSpec((rows, C), lambda *ids: (step_of(*ids), 0)),
            jax.ShapeDtypeStruct((R, C), BF))


def ffn_gate_up(x, w, layer, w_down, tn=512, tm=1024):
    M, K = x.shape
    nj, ni = D_FF // tn, M // tm
    cast_in, cast_out, cast_shape = _cast_rider(w_down, layer, nj * ni, lambda j, i: j * ni + i)
    return pl.pallas_call(
        _gu_kernel,
        grid=(nj, ni),
        in_specs=[pl.BlockSpec((tm, K), lambda j, i: (i, 0)),
                  pl.BlockSpec((None, K, tn), lambda j, i: (layer, 0, j)),
                  pl.BlockSpec((None, K, tn), lambda j, i: (layer, 0, j + nj)),
                  cast_in],
        out_specs=[pl.BlockSpec((tm, tn), lambda j, i: (i, j)), cast_out],
        out_shape=[jax.ShapeDtypeStruct((M, D_FF), BF), cast_shape],
        scratch_shapes=[pltpu.VMEM((K, tn), BF), pltpu.VMEM((K, tn), BF)],
        compiler_params=_params("arbitrary", "arbitrary"),
        name="ffn_gate_up",
    )(x, w, w, w_down)


def _proj_res_kernel(has_x2, n_next, n_sub, *refs):
    refs = list(refs)
    x1_ref = refs.pop(0)
    x2_ref = refs.pop(0) if has_x2 else None
    w_ref, xres_ref, g_ref = refs.pop(0), refs.pop(0), refs.pop(0)
    gn_ref = refs.pop(0) if n_next else None
    xo_ref = refs.pop(0)
    uo_refs = [refs.pop(0) for _ in range(n_next)]
    k1 = x1_ref.shape[1]
    sub = xo_ref.shape[0] // n_sub
    for h in range(n_sub):
        rows = slice(h * sub, (h + 1) * sub)
        acc = jnp.dot(x1_ref[rows, :], w_ref[:k1, :], preferred_element_type=F32)
        if has_x2:
            acc = acc + jnp.dot(x2_ref[rows, :], w_ref[k1:, :], preferred_element_type=F32)
        xn = xres_ref[rows, :] + _rms(acc, g_ref[...])
        xo_ref[rows, :] = xn
        if n_next:
            y = xn * lax.rsqrt(jnp.mean(xn * xn, axis=-1, keepdims=True) + EPS)
            for t, uo_ref in enumerate(uo_refs):
                uo_ref[rows, :] = (y * gn_ref[t:t + 1, :]).astype(uo_ref.dtype)


def proj_residual_norm(x1, x2, x2_col0, k2, w, layer, xres, g, gn, tm=512, n_sub=2):
    M, K1 = x1.shape
    K, N = w.shape[1], w.shape[2]
    has_x2 = x2 is not None
    n_next = 0 if gn is None else gn.shape[0]
    assert K == K1 + (k2 if has_x2 else 0)
    row = lambda i: (i, 0)
    in_specs = [pl.BlockSpec((tm, K1), row)]
    args = [x1]
    if has_x2:
        assert x2_col0 % k2 == 0
        in_specs.append(pl.BlockSpec((tm, k2), lambda i: (i, x2_col0 // k2)))
        args.append(x2)
    in_specs += [pl.BlockSpec((None, K, N), lambda i: (layer, 0, 0), pipeline_mode=pl.Buffered(1)),
                 pl.BlockSpec((tm, N), row),
                 pl.BlockSpec((1, N), lambda i: (0, 0))]
    args += [w, xres, g.reshape(1, N)]
    out_specs = [pl.BlockSpec((tm, N), row)]
    out_shape = [jax.ShapeDtypeStruct((M, N), F32)]
    if n_next:
        in_specs.append(pl.BlockSpec((n_next, N), lambda i: (0, 0)))
        args.append(gn)
        out_specs += [pl.BlockSpec((tm, N), row)] * n_next
        out_shape += [jax.ShapeDtypeStruct((M, N), BF)] * n_next
    res = pl.pallas_call(
        functools.partial(_proj_res_kernel, has_x2, n_next, n_sub),
        grid=(M // tm,),
        in_specs=in_specs,
        out_specs=out_specs,
        out_shape=out_shape,
        compiler_params=_params("arbitrary"),
        name="proj_residual_norm",
    )(*args)
    return res[0], list(res[1:])


def _divmod_pow2(n, d):
    assert d & (d - 1) == 0
    return lax.shift_right_logical(n, d.bit_length() - 1), n & (d - 1)


def _dil_kernel(*refs):
    ng = len(DIL_GROUPS)
    q_refs, k_refs, v_refs = refs[:ng], refs[ng:2 * ng], refs[2 * ng:3 * ng]
    o_ref, o_scr, l_scr, stage_scr, ostage_scr = refs[3 * ng:]
    S = o_ref.shape[1]
    blk = A_BLOCK
    n_heads = ng * A_HEADS_PER_GROUP
    j = pl.program_id(1)
    for gi, (w, d) in enumerate(DIL_GROUPS):
        q_ref, k_ref, v_ref = q_refs[gi], k_refs[gi], v_refs[gi]
        nqb = S // d // blk
        max_dist = w // d
        has_prev = nqb > 1
        nk = 2 * blk if has_prev else blk
        koff = blk if has_prev else 0
        head = (j + (gi * A_HEADS_PER_GROUP + 1)).astype(F32)
        slope = jnp.exp2(jnp.zeros((1, 1), F32) + head * (-8.0 / n_heads))
        kidx = _iota((blk, nk), 1)
        dist = _iota((blk, nk), 0) + koff - kidx
        band = (dist >= 0) & (dist <= max_dist)
        bias = jnp.where(band, (kidx - koff).astype(F32) * (slope * (d * LOG2E)), NEG)
        bias_first = jnp.where(kidx >= koff, bias, NEG)
        row_term = _iota((blk, 1), 0).astype(F32) * (slope * (-float(d)))

        staged = d > STAGE_STRIDE
        if staged:
            assert d % STAGE_STRIDE == 0 and d // STAGE_STRIDE <= STAGE_STRIDE
            quarter = S // STAGE_STRIDE
            for t_idx, ref in enumerate((q_ref, k_ref, v_ref)):
                for r1 in range(STAGE_STRIDE):
                    stage_scr[t_idx, r1 * quarter:(r1 + 1) * quarter, :] = ref[0, pl.ds(r1, quarter, stride=STAGE_STRIDE), :]

        def out_rows(r, i, d=d):
            start = r + i * (d * blk)
            return pl.ds(start, blk, stride=d) if d > 1 else pl.ds(start, blk)

        def staged_rows(r, i, d=d):
            s2 = d // STAGE_STRIDE
            r2, r1 = _divmod_pow2(r, STAGE_STRIDE)
            return pl.ds(r1 * (S // STAGE_STRIDE) + r2 + i * (s2 * blk), blk, stride=s2)

        def load(t_idx, ref, r, i, staged=staged):
            return stage_scr[t_idx, staged_rows(r, i), :] if staged else ref[0, out_rows(r, i), :]

        def body(n0, carry, gi=gi, d=d, has_prev=has_prev, q_ref=q_ref, k_ref=k_ref, v_ref=v_ref,
                 bias=bias, bias_first=bias_first, row_term=row_term, out_rows=out_rows, load=load,
                 staged=staged, staged_rows=staged_rows):
            ri = [_divmod_pow2(n0 * DIL_GROUP_BLOCKS + g, d)[::-1] for g in range(DIL_GROUP_BLOCKS)]
            qs = [load(0, q_ref, r, i).astype(BF) for r, i in ri]
            if has_prev:
                ips = [jnp.maximum(i - 1, 0) for _, i in ri]
                ks = [jnp.concatenate([load(1, k_ref, r, ip), load(1, k_ref, r, i)], axis=0).astype(BF)
                      for (r, i), ip in zip(ri, ips)]
                vs = [jnp.concatenate([load(2, v_ref, r, ip), load(2, v_ref, r, i)], axis=0).astype(BF)
                      for (r, i), ip in zip(ri, ips)]
                bs = [jnp.where(i > 0, bias, bias_first) for _, i in ri]
            else:
                ks = [load(1, k_ref, r, i).astype(BF) for r, i in ri]
                vs = [load(2, v_ref, r, i).astype(BF) for r, i in ri]
                bs = [bias] * len(ri)
            ss = [lax.dot_general(q, k, NT, preferred_element_type=F32) + b for q, k, b in zip(qs, ks, bs)]
            ms = [jnp.max(s, axis=-1, keepdims=True) for s in ss]
            ps = [jnp.exp2(s - m) for s, m in zip(ss, ms)]
            ls = [jnp.sum(p, axis=-1, keepdims=True) for p in ps]
            os_ = [jnp.dot(p.astype(BF), v, preferred_element_type=F32) * (1.0 / l) for p, v, l in zip(ps, vs, ls)]
            for (r, i), o, m, l in zip(ri, os_, ms, ls):
                lse_b = jnp.broadcast_to((m + jnp.log2(l)) * LN2 + row_term, (blk, HEAD_DIM))
                if staged:
                    rows = staged_rows(r, i)
                    ostage_scr[0, rows, :] = o
                    ostage_scr[1, rows, :] = lse_b
                else:
                    rows = out_rows(r, i)
                    o_scr[gi, rows, :] = o
                    l_scr[gi, rows, :] = lse_b
            return carry

        assert (d * nqb) % DIL_GROUP_BLOCKS == 0
        lax.fori_loop(0, d * nqb // DIL_GROUP_BLOCKS, body, 0)
        if staged:
            for r1 in range(STAGE_STRIDE):
                dst = pl.ds(r1, quarter, stride=STAGE_STRIDE)
                o_scr[gi, dst, :] = ostage_scr[0, r1 * quarter:(r1 + 1) * quarter, :]
                l_scr[gi, dst, :] = ostage_scr[1, r1 * quarter:(r1 + 1) * quarter, :]

    tr = 256
    for c in range(S // tr):
        rows = slice(c * tr, (c + 1) * tr)
        ls = [l_scr[gi, rows, :] for gi in range(ng)]
        m = functools.reduce(jnp.maximum, ls)
        es = [jnp.exp(x - m) for x in ls]
        num = sum(e * o_scr[gi, rows, :] for gi, e in enumerate(es))
        o_ref[0, rows, :] = (num * (1.0 / sum(es))).astype(o_ref.dtype)


def dilated_attention(q, k, v):
    B, S, _ = q.shape
    ng, Hg = len(DIL_GROUPS), A_HEADS_PER_GROUP
    for w, d in DIL_GROUPS:
        assert w // d <= A_BLOCK and S % (d * A_BLOCK) == 0
    specs = [pl.BlockSpec((1, S, HEAD_DIM), lambda b, j, gi=gi: (b, 0, gi * Hg + j)) for gi in range(ng)]
    return pl.pallas_call(
        _dil_kernel,
        grid=(B, Hg),
        in_specs=specs * 3,
        out_specs=pl.BlockSpec((1, S, HEAD_DIM), lambda b, j: (b, 0, j)),
        out_shape=jax.ShapeDtypeStruct((B, S, Hg * HEAD_DIM), BF),
        scratch_shapes=[pltpu.VMEM((ng, S, HEAD_DIM), F32), pltpu.VMEM((ng, S, HEAD_DIM), F32),
                        pltpu.VMEM((3, S, HEAD_DIM), F32), pltpu.VMEM((2, S, HEAD_DIM), F32)],
        compiler_params=_params("arbitrary", "arbitrary"),
        name="dilated_attention",
    )(*([q] * ng + [k] * ng + [v] * ng))


def _mem_attn_kernel(q_ref, kv_ref, o_ref):
    for h in range(MEM_HEADS):
        c = slice(h * HEAD_DIM, (h + 1) * HEAD_DIM)
        k = kv_ref[0, :, c]
        v = kv_ref[0, :, MEM_HEADS * HEAD_DIM + h * HEAD_DIM:MEM_HEADS * HEAD_DIM + (h + 1) * HEAD_DIM]
        s = lax.dot_general(q_ref[0, :, c], k, NT, preferred_element_type=F32) * SCALE
        m = jnp.max(s, axis=-1, keepdims=True)
        e = jnp.exp(s - m)
        p = e * (1.0 / jnp.sum(e, axis=-1, keepdims=True))
        o_ref[0, :, c] = jnp.dot(p.astype(BF), v, preferred_element_type=F32).astype(o_ref.dtype)


def memory_attention(q, q_col0, kv, layer, tq=512):
    B, S, _ = q.shape
    w = MEM_HEADS * HEAD_DIM
    cb = q_col0 // w
    return pl.pallas_call(
        _mem_attn_kernel,
        grid=(B, S // tq),
        in_specs=[pl.BlockSpec((1, tq, w), lambda b, i: (b, i, cb)),
                  pl.BlockSpec((None, 1, N_MEM, 2 * w), lambda b, i: (layer, b, 0, 0))],
        out_specs=pl.BlockSpec((1, tq, w), lambda b, i: (b, i, 0)),
        out_shape=jax.ShapeDtypeStruct((B, S, w), BF),
        compiler_params=_params("arbitrary", "arbitrary"),
        name="memory_attention",
    )(q, kv)


def _gelu_tanh(x):
    return 0.5 * x * (1.0 + jnp.tanh(math.sqrt(2.0 / math.pi) * (x + 0.044715 * (x * x * x))))


def _compress_kernel(out_scale, x_ref, w1_ref, w2_ref, pe_ref, o_ref, w1b_ref):
    @pl.when((pl.program_id(0) == 0) & (pl.program_id(1) == 0))
    def _():
        w1b_ref[...] = w1_ref[...].astype(BF)

    nchunk = o_ref.shape[1]
    top = jnp.zeros((nchunk, CMP_HIDDEN), F32)
    bot = jnp.zeros((nchunk, CMP_HIDDEN), F32)
    for l in range(CMP_STRIDE):
        xl = x_ref[0, pl.ds(l, nchunk, stride=CMP_STRIDE), :]
        lo, hi = l, CMP_STRIDE + l
        top += jnp.dot((xl + pe_ref[lo:lo + 1, :]).astype(BF),
                       w1b_ref[lo * HEAD_DIM:(lo + 1) * HEAD_DIM, :], preferred_element_type=F32)
        bot += jnp.dot((xl + pe_ref[hi:hi + 1, :]).astype(BF),
                       w1b_ref[hi * HEAD_DIM:(hi + 1) * HEAD_DIM, :], preferred_element_type=F32)
    hid = _gelu_tanh(top + pltpu.roll(bot, nchunk - 1, 0))
    out = jnp.dot(hid.astype(BF), w2_ref[...].astype(BF), preferred_element_type=F32)
    o_ref[0] = (out * out_scale).astype(o_ref.dtype)


def compress_blocks(kvc, sec, w1, w2, pe, out_scale=1.0):
    B, S, C = kvc.shape
    assert CMP_LEN == 2 * CMP_STRIDE
    G = B_KV_GROUPS
    nchunk = S // CMP_STRIDE
    return pl.pallas_call(
        functools.partial(_compress_kernel, out_scale),
        grid=(B, G),
        in_specs=[pl.BlockSpec((1, S, HEAD_DIM), lambda b, g: (b, 0, sec * G + g)),
                  pl.BlockSpec(w1.shape, lambda b, g: (0, 0)),
                  pl.BlockSpec(w2.shape, lambda b, g: (0, 0)),
                  pl.BlockSpec(pe.shape, lambda b, g: (0, 0))],
        out_specs=pl.BlockSpec((1, nchunk, HEAD_DIM), lambda b, g: (b, 0, g)),
        out_shape=jax.ShapeDtypeStruct((B, nchunk, G * HEAD_DIM), BF),
        scratch_shapes=[pltpu.VMEM(w1.shape, BF)],
        compiler_params=_params("arbitrary", "arbitrary"),
        name=f"compress_{sec}",
    )(kvc, w1, w2, pe)


ONES_ROWS = 16
AUG_K = LANES
NEG_ROWS = 16


def _key_aug(nk, bps):
    idx = _iota((nk, AUG_K), 0)
    lane = _iota((nk, AUG_K), 1)
    piece = jnp.where((lane - NEG_ROWS) % 2 == 0, (idx // 256) * 256, idx % 256)
    val = jnp.where(lane < bps, (idx // SLC_LEN == lane).astype(jnp.int32),
                    jnp.where((lane >= NEG_ROWS) & (lane < NEG_ROWS + 6), piece, 0))
    return val.astype(F32).astype(BF)


def _offset_masks(n, nk, tq, lo, hi):
    dist = (_iota((n, nk, tq), 0) * tq + _iota((n, nk, tq), 2)) - _iota((n, nk, tq), 1)
    return jnp.where((dist >= lo) & (dist <= hi), 0.0, NEG).astype(F32)


def _t_bf(x):
    return x.astype(F32).T.astype(BF)


def _nsa_kernel(tq, q_ref, gate_ref, kc_ref, vc_ref, ks_ref, vs_ref, kw_ref, vw_ref,
                sl_ref, kaug_w_ref, kaug_s_ref, wmask_ref, cmask_ref, o_ref,
                vst_ref, vwt_ref, qt_ref, ocmp_ref, neg_ref, raw_a_ref, raw_b_ref, pc_ref):
    S = ks_ref.shape[1]
    ncmp = kc_ref.shape[1]
    nsel = S // SLC_LEN
    i = pl.program_id(2)

    t0 = i * tq
    tw = B_REP * tq
    nt = S // tq
    n_back = -(-(WIN_LEN - 1) // tq)
    nwin = (n_back + 1) * tq
    kcs = vst_ref.shape[2]
    bps = kcs // SLC_LEN

    def per_head(fn, n=B_REP):
        return jnp.concatenate([fn(r) for r in range(n)], axis=1)

    def bf_exact(x):
        return x.astype(BF).astype(F32)

    @pl.when(i == 0)
    def _():
        for src_ref, dst_ref in ((vs_ref, vst_ref), (vw_ref, vwt_ref)):
            n, _, w = dst_ref.shape
            ones = jnp.ones((ONES_ROWS, w), BF)
            for c in range(n):
                dst_ref[c] = jnp.concatenate([_t_bf(src_ref[0, c * w:(c + 1) * w, :]), ones], axis=0)
        vct = _t_bf(vc_ref[0])
        for t in range(nt):
            for r in range(B_REP):
                qt_ref[t, r] = _t_bf(q_ref[0, t * tq:(t + 1) * tq, r * HEAD_DIM:(r + 1) * HEAD_DIM])
        q_w = jnp.concatenate([qt_ref[t, r] for r in range(B_REP) for t in range(nt)], axis=1)
        sl2_w = per_head(lambda r: jnp.broadcast_to(sl_ref[0, 0:1, r:r + 1], (1, S))) * LOG2E
        pos = _iota((1, S), 1)
        pos_w = per_head(lambda r: pos)

        dist_c = pos_w - (_iota((ncmp, 1), 0) * CMP_STRIDE + (CMP_LEN - 1))
        s = (jnp.dot(kc_ref[0], q_w, preferred_element_type=F32)
             + jnp.where(dist_c >= 0, dist_c.astype(F32) * (-sl2_w), NEG))
        e = jnp.exp2(s - jnp.max(s, axis=0, keepdims=True))
        has_block = (pos_w >= CMP_LEN - 1).astype(F32)
        p = e * (has_block / jnp.maximum(jnp.sum(e, axis=0, keepdims=True), 1e-30))
        o_cmp_w = jnp.dot(vct, p.astype(BF), preferred_element_type=F32)
        for t in range(nt):
            for r in range(B_REP):
                ocmp_ref[t, r] = o_cmp_w[:, r * S + t * tq:r * S + (t + 1) * tq]

        ratio = SLC_LEN // CMP_STRIDE
        assert SLC_LEN % CMP_STRIDE == 0 and CMP_LEN <= SLC_LEN and ncmp >= nsel * ratio
        p_heads = sum(p[:, r * S:(r + 1) * S] for r in range(B_REP))
        for lt in range(S // LANES):
            pc_ref[lt] = p_heads[:, lt * LANES:(lt + 1) * LANES]
        imp = jnp.zeros((nsel, S), F32)
        for o in range(1 - (-(-CMP_LEN // CMP_STRIDE)), ratio):
            overlap = max(min(o * CMP_STRIDE + CMP_LEN, SLC_LEN) - max(o * CMP_STRIDE, 0), 0)
            if overlap == 0:
                continue
            part = jnp.concatenate([pc_ref[lt, pl.ds(o % ratio, nsel, stride=ratio), :]
                                    for lt in range(S // LANES)], axis=1)
            if o < 0:
                part = jnp.where(_iota((nsel, S), 0) == 0, 0.0, pltpu.roll(part, 1, 0))
            imp = imp + (overlap / CMP_LEN) * part

        jb = _iota((nsel, S), 0)
        cur = pos // SLC_LEN
        forced = (jb == 0) | (jb == cur) | (jb == cur - 1)
        imp = jnp.where(forced, SEL_FORCE, jnp.where(jb > cur, -SEL_FORCE, imp))
        beaten_by = jnp.zeros((nsel, S), jnp.int32)
        for j in range(nsel):
            row = imp[j:j + 1, :]
            beats = (row > imp) | ((row == imp) & (jb > j))
            beaten_by = beaten_by + beats.astype(jnp.int32)
        neg_w = jnp.where(beaten_by >= min(SLC_TOPK, nsel), NEG, 0.0)
        for t in range(nt):
            neg_ref[t] = neg_w[:, t * tq:(t + 1) * tq]

    q_all = per_head(lambda r: qt_ref[i, r])
    o_cmp = per_head(lambda r: ocmp_ref[i, r])
    sl_row = per_head(lambda r: jnp.broadcast_to(sl_ref[0, 0:1, r:r + 1], (1, tq)))
    sl2_row = sl_row * LOG2E

    c1 = bf_exact(sl2_row)
    c2 = bf_exact(sl2_row - c1)
    c3 = bf_exact(sl2_row - c1 - c2)
    slope_rows = jnp.concatenate([c1, c1, c2, c2, c3, c3, jnp.zeros((AUG_K - NEG_ROWS - 6, tw), F32)],
                                 axis=0).astype(BF)

    def scores(k_c, k_aug, neg_rows):
        q_full = jnp.concatenate([q_all, neg_rows, slope_rows], axis=0)
        return jnp.dot(jnp.concatenate([k_c, k_aug], axis=1), q_full, preferred_element_type=F32)

    def neg_rows_of(n):
        rows = neg_ref[i, pl.ds(pl.multiple_of(n * bps, bps), bps), :]
        return jnp.concatenate([per_head(lambda r: rows), jnp.zeros((NEG_ROWS - bps, tw), F32)], axis=0).astype(BF)

    no_rows = jnp.zeros((NEG_ROWS, tw), BF)
    c0 = jnp.maximum(i - n_back, 0)
    wbase = pl.multiple_of(c0 * tq, tq)
    raw_w = scores(kw_ref[0, pl.ds(wbase, nwin), :], kaug_w_ref[...], no_rows)
    k_aug = kaug_s_ref[...]
    n_last = (t0 + tq - 1) // kcs
    raw_d = scores(ks_ref[0, pl.ds(pl.multiple_of(n_last * kcs, kcs), kcs), :], k_aug, no_rows)

    def softmax_step(raw, vt_c, mask, c_off, st):
        m, acc = st
        s = raw if mask is None else raw + per_head(lambda r: mask)
        m_new = jnp.maximum(m, jnp.max(s, axis=0, keepdims=True) + c_off)
        p = jnp.exp2(s - (m_new - c_off))
        acc = jnp.exp2(m - m_new) * acc + jnp.dot(vt_c, p.astype(BF), preferred_element_type=F32)
        return m_new, acc

    def finish(st):
        _, acc = st
        return acc[:HEAD_DIM] * (1.0 / acc[HEAD_DIM:HEAD_DIM + 1])

    init = (jnp.full((1, tw), NEG, F32), jnp.zeros((HEAD_DIM + ONES_ROWS, tw), F32))
    no_off = jnp.zeros((1, 1), F32)

    vwt = jnp.concatenate([vwt_ref[c0 + j] for j in range(n_back + 1)], axis=1)
    o_win = finish(softmax_step(raw_w, vwt, wmask_ref[i - c0], no_off, init))

    def slab_scores(n):
        return scores(ks_ref[0, pl.ds(pl.multiple_of(n * kcs, kcs), kcs), :], k_aug, neg_rows_of(n))

    raw_a_ref[...] = slab_scores(0)
    sel_rows = [jnp.broadcast_to(neg_ref[i, pl.ds(n_last * bps + b, 1), :], (SLC_LEN, tq)) for b in range(bps)]
    mask_d = jnp.concatenate(sel_rows, axis=0) + cmask_ref[i - n_last * (kcs // tq)]
    st = softmax_step(raw_d, vst_ref[n_last], mask_d, (n_last * kcs - t0).astype(F32) * sl2_row, init)

    def sel_slab(n, st, src_ref, dst_ref):
        dst_ref[...] = slab_scores(jnp.minimum(n + 1, n_last))
        return softmax_step(src_ref[...], vst_ref[n], None, (n * kcs - t0).astype(F32) * sl2_row, st)

    def sel_pair(j, st):
        st = sel_slab(2 * j, st, raw_a_ref, raw_b_ref)
        return lax.cond(2 * j + 1 < n_last, lambda s: sel_slab(2 * j + 1, s, raw_b_ref, raw_a_ref),
                        lambda s: s, st)

    o_slc = finish(lax.fori_loop(0, (n_last + 1) // 2, sel_pair, st))

    gate_t = (1.0 / (1.0 + jnp.exp(-gate_ref[0]))).T
    for r in range(B_REP):
        c = slice(r * tq, (r + 1) * tq)
        o_t = (gate_t[3 * r:3 * r + 1] * o_cmp[:, c] + gate_t[3 * r + 1:3 * r + 2] * o_slc[:, c]
               + gate_t[3 * r + 2:3 * r + 3] * o_win[:, c])
        o_ref[0, :, r * HEAD_DIM:(r + 1) * HEAD_DIM] = o_t.T.astype(o_ref.dtype)


def nsa_attention(qp, gates, kc, vc, kv, tq=256):
    B, S, _ = qp.shape
    G = B_KV_GROUPS
    gq = B_REP * HEAD_DIM
    ncmp = kc.shape[1]
    kcs = 2 * tq
    nsel = S // SLC_LEN
    assert kcs % SLC_LEN == 0 and S % kcs == 0 and S >= (-(-(WIN_LEN - 1) // tq) + 1) * tq
    slopes = 2.0 ** (-8.0 * jnp.arange(1, B_HEADS + 1, dtype=F32) / B_HEADS)
    sl = jnp.zeros((G, 8, LANES), F32).at[:, :, :B_REP].set(
        jnp.broadcast_to(slopes.reshape(G, 1, B_REP), (G, 8, B_REP)))
    kvspec = lambda sec: pl.BlockSpec((1, S, HEAD_DIM), lambda b, g, i: (b, 0, sec * G + g))
    cspec = pl.BlockSpec((1, ncmp, HEAD_DIM), lambda b, g, i: (b, 0, g))
    vt_rows = HEAD_DIM + ONES_ROWS
    n_back = -(-(WIN_LEN - 1) // tq)
    nwin = (n_back + 1) * tq
    bps = kcs // SLC_LEN
    assert bps <= NEG_ROWS and NEG_ROWS + 6 <= AUG_K
    consts = [_key_aug(nwin, bps), _key_aug(kcs, bps), _offset_masks(n_back + 1, nwin, tq, 0, WIN_LEN - 1),
              _offset_masks(kcs // tq, kcs, tq, 0, S)]
    whole = lambda a: pl.BlockSpec(a.shape, lambda b, g, i, nd=a.ndim: (0,) * nd)
    return pl.pallas_call(
        functools.partial(_nsa_kernel, tq),
        grid=(B, G, S // tq),
        in_specs=[pl.BlockSpec((1, S, gq), lambda b, g, i: (b, 0, g)),
                  pl.BlockSpec((1, tq, LANES), lambda b, g, i: (b, i, g)),
                  cspec, cspec, kvspec(0), kvspec(1), kvspec(2), kvspec(3),
                  pl.BlockSpec((1, 8, LANES), lambda b, g, i: (g, 0, 0))] + [whole(a) for a in consts],
        out_specs=pl.BlockSpec((1, tq, gq), lambda b, g, i: (b, i, g)),
        out_shape=jax.ShapeDtypeStruct((B, S, B_HEADS * HEAD_DIM), BF),
        scratch_shapes=[pltpu.VMEM((S // kcs, vt_rows, kcs), BF), pltpu.VMEM((S // tq, vt_rows, tq), BF),
                        pltpu.VMEM((S // tq, B_REP, HEAD_DIM, tq), BF), pltpu.VMEM((S // tq, B_REP, HEAD_DIM, tq), F32),
                        pltpu.VMEM((S // tq, nsel, tq), F32),
                        pltpu.VMEM((kcs, B_REP * tq), F32), pltpu.VMEM((kcs, B_REP * tq), F32),
                        pltpu.VMEM((S // LANES, ncmp, LANES), F32)],
        compiler_params=_params("arbitrary", "arbitrary", "arbitrary"),
        name="nsa_attention",
    )(qp, gates, kc, vc, kv, kv, kv, kv, sl, *consts)


def kernel(x, mem, norm_g, a_w_in, a_w_out, b_w_in, b_w_out, mem_w_kv, ffn_w_gu, ffn_w_down,
           kv_norm_g, kv_w, cmp_pe, cmp_wk1, cmp_wk2, cmp_wv1, cmp_wv2):
    B, S, D = x.shape
    M = B * S
    n_a = DEPTH // 2
    qa_cols = len(DIL_GROUPS) * A_HEADS_PER_GROUP * HEAD_DIM
    qb_cols = B_HEADS * HEAD_DIM
    mq_cols = MEM_HEADS * HEAD_DIM
    n_gate = 3 * B_HEADS
    G = B_KV_GROUPS

    b_w_mq = b_w_in[:, :, qb_cols + n_gate:]
    b_w_gate = jnp.pad(b_w_in[:, :, qb_cols:qb_cols + n_gate].reshape(-1, D, G, n_gate // G),
                       ((0, 0), (0, 0), (0, 0), (0, LANES - n_gate // G))).reshape(-1, D, G * LANES)
    kv_w3 = kv_w[None]

    x2 = x.reshape(M, D)
    mkv = memory_kv_all_layers(mem.reshape(B * N_MEM, D), norm_g[:, 4], mem_w_kv).reshape(
        DEPTH, B, N_MEM, 2 * mq_cols)
    u = rmsnorm_rows(x2, norm_g[0, 0])
    shared = None
    for l in range(DEPTH):
        g = norm_g[l]
        if l < n_a:
            q, w_out = matmul(u, a_w_in, l, col0=0, ncols=qa_cols, out_dtype=F32, cast=(a_w_out, l), name="a_q")
            k = matmul(u, a_w_in, l, col0=qa_cols, ncols=qa_cols, out_dtype=F32,
                       col_scale=jnp.full((qa_cols,), SCALE2, F32), name="a_k")
            v = matmul(u, a_w_in, l, col0=2 * qa_cols, ncols=qa_cols, out_dtype=F32, name="a_v")
            mq = matmul(u, a_w_in, l, col0=3 * qa_cols, ncols=mq_cols, name="a_mq").reshape(B, S, mq_cols)
            o_main = dilated_attention(*(t.reshape(B, S, qa_cols) for t in (q, k, v)))
            o_mem = memory_attention(mq, 0, mkv, l)
        else:
            li = l - n_a
            if shared is None:
                kvc = matmul(kvn, kv_w3, 0, col0=0, ncols=2 * G * HEAD_DIM, out_dtype=F32,
                             name="kv_cmp").reshape(B, S, 2 * G * HEAD_DIM)
                sec = G * HEAD_DIM
                key_scale = jnp.tile(jnp.concatenate([jnp.full((sec,), SCALE2, F32), jnp.ones((sec,), F32)]), 2)
                kvr = matmul(kvn, kv_w3, 0, col0=2 * G * HEAD_DIM, ncols=4 * G * HEAD_DIM, col_scale=key_scale,
                             name="kv_rest").reshape(B, S, 4 * G * HEAD_DIM)
                kc = compress_blocks(kvc, 0, cmp_wk1, cmp_wk2, cmp_pe[0], out_scale=SCALE2)
                vc = compress_blocks(kvc, 1, cmp_wv1, cmp_wv2, cmp_pe[1])
                shared = (kc, vc, kvr)
            qb = matmul(u, b_w_in, li, col0=0, ncols=qb_cols, name="b_q").reshape(B, S, qb_cols)
            mq = matmul(u, b_w_mq, li, name="b_mq").reshape(B, S, mq_cols)
            gates, w_out = matmul(u, b_w_gate, li, out_dtype=F32, cast=(b_w_out, li), name="b_gate")
            o_main = nsa_attention(qb, gates.reshape(B, S, G * LANES), *shared)
            o_mem = memory_attention(mq, 0, mkv, l)
        x2, (u,) = proj_residual_norm(o_main.reshape(M, -1), o_mem.reshape(M, mq_cols), 0, mq_cols,
                                      w_out[None], 0, x2, g[1], g[2:3])
        h, w_down = ffn_gate_up(u, ffn_w_gu, l, ffn_w_down)
        next_gains = [norm_g[l + 1, 0]] if l + 1 < DEPTH else []
        if l + 1 == n_a:
            next_gains.append(kv_norm_g)
        x2, us = proj_residual_norm(h, None, 0, 0, w_down[None], 0, x2, g[3],
                                    jnp.stack(next_gains) if next_gains else None, tm=256, n_sub=2)
        u = us[0] if us else None
        if l + 1 == n_a:
            kvn = us[1]
    return x2.reshape(B, S, D)
```

```python
import functools
import math

import jax
import jax.numpy as jnp
from jax import lax
from jax.experimental import pallas as pl
from jax.experimental.pallas import tpu as pltpu

BF = jnp.bfloat16
F32 = jnp.float32

D_MODEL = 2048
DEPTH = 4
HEAD_DIM = 128
N_MEM = 256
MEM_HEADS = 4
DIL_GROUPS = ((128, 1), (512, 4), (2048, 16))
A_HEADS_PER_GROUP = 8
A_BLOCK = 128
B_HEADS = 12
B_KV_GROUPS = 4
B_REP = B_HEADS // B_KV_GROUPS
CMP_LEN = 32
CMP_STRIDE = 16
CMP_HIDDEN = 512
SLC_LEN = 64
SLC_TOPK = 16
SEL_FORCE = 1e4
WIN_LEN = 512
D_FF = 5632
EPS = 1e-6

LANES = 128
VMEM_LIMIT = 56 * 1024 * 1024
NEG = -1e30
SCALE = 1.0 / math.sqrt(HEAD_DIM)
LOG2E = math.log2(math.e)
LN2 = math.log(2.0)
SCALE2 = SCALE * LOG2E
NT = (((1,), (1,)), ((), ()))
DIL_GROUP_BLOCKS = 16
STAGE_STRIDE = 4


def _params(*sem):
    return pltpu.CompilerParams(dimension_semantics=sem, vmem_limit_bytes=VMEM_LIMIT)


def _rms(x, g):
    return x * lax.rsqrt(jnp.mean(x * x, axis=-1, keepdims=True) + EPS) * g


def _iota(shape, dim):
    return lax.broadcasted_iota(jnp.int32, shape, dim)


def _rmsnorm_kernel(x_ref, g_ref, o_ref):
    o_ref[...] = _rms(x_ref[...], g_ref[...]).astype(o_ref.dtype)


def rmsnorm_rows(x, g, tm=1024):
    M, D = x.shape
    tm = min(tm, M)
    return pl.pallas_call(
        _rmsnorm_kernel,
        grid=(M // tm,),
        in_specs=[pl.BlockSpec((tm, D), lambda i: (i, 0)),
                  pl.BlockSpec((1, D), lambda i: (0, 0))],
        out_specs=pl.BlockSpec((tm, D), lambda i: (i, 0)),
        out_shape=jax.ShapeDtypeStruct((M, D), BF),
        compiler_params=_params("arbitrary"),
        name="rmsnorm",
    )(x, g.reshape(1, D))


def _mem_kv_kernel(x_ref, g_ref, w_ref, o_ref):
    xn = _rms(x_ref[...], g_ref[...]).astype(BF)
    o_ref[...] = jnp.dot(xn, w_ref[...].astype(BF), preferred_element_type=F32).astype(o_ref.dtype)


def memory_kv_all_layers(mem2, g, w, tn=512):
    Mm, D = mem2.shape
    L, _, N = w.shape
    return pl.pallas_call(
        _mem_kv_kernel,
        grid=(L, N // tn),
        in_specs=[pl.BlockSpec((Mm, D), lambda l, j: (0, 0)),
                  pl.BlockSpec((None, 1, D), lambda l, j: (l, 0, 0)),
                  pl.BlockSpec((None, D, tn), lambda l, j: (l, 0, j))],
        out_specs=pl.BlockSpec((None, Mm, tn), lambda l, j: (l, 0, j)),
        out_shape=jax.ShapeDtypeStruct((L, Mm, N), BF),
        compiler_params=_params("arbitrary", "arbitrary"),
        name="mem_kv",
    )(mem2, g.reshape(L, 1, D), w)


def _mm_kernel(has_scale, has_cast, *refs):
    refs = list(refs)
    x_ref, w_ref = refs.pop(0), refs.pop(0)
    s_ref = refs.pop(0) if has_scale else None
    cast_ref = refs.pop(0) if has_cast else None
    o_ref = refs.pop(0)
    cast_o_ref = refs.pop(0) if has_cast else None
    wbf_ref = refs.pop(0)
    j = pl.program_id(1)

    @pl.when(pl.program_id(0) == 0)
    def _():
        wbf_ref[j] = w_ref[...].astype(BF)

    acc = jnp.dot(x_ref[...], wbf_ref[j], preferred_element_type=F32)
    if has_scale:
        acc = acc * s_ref[...]
    o_ref[...] = acc.astype(o_ref.dtype)
    if has_cast:
        cast_o_ref[...] = cast_ref[...].astype(cast_o_ref.dtype)


def matmul(x, w, layer, *, col0=0, ncols=None, tm=1024, out_dtype=BF, col_scale=None, cast=None,
           name="matmul"):
    M, K = x.shape
    ncols = w.shape[2] - col0 if ncols is None else ncols
    tn = 1024 if (ncols % 1024 == 0 and col0 % 1024 == 0) else 512
    tm = min(tm, M)
    assert col0 % tn == 0 and ncols % tn == 0 and M % tm == 0
    c0 = col0 // tn
    nj = ncols // tn
    in_specs = [pl.BlockSpec((tm, K), lambda i, j: (i, 0)),
                pl.BlockSpec((None, K, tn), lambda i, j: (layer, 0, jnp.where(i == 0, j, 0) + c0))]
    args = [x, w]
    out_specs = [pl.BlockSpec((tm, tn), lambda i, j: (i, j))]
    out_shape = [jax.ShapeDtypeStruct((M, ncols), out_dtype)]
    if col_scale is not None:
        in_specs.append(pl.BlockSpec((1, tn), lambda i, j: (0, j)))
        args.append(col_scale.reshape(1, ncols))
    if cast is not None:
        cast_in, cast_out, cast_shape = _cast_rider(cast[0], cast[1], (M // tm) * nj, lambda i, j: i * nj + j)
        in_specs.append(cast_in)
        args.append(cast[0])
        out_specs.append(cast_out)
        out_shape.append(cast_shape)
    res = pl.pallas_call(
        functools.partial(_mm_kernel, col_scale is not None, cast is not None),
        grid=(M // tm, nj),
        in_specs=in_specs,
        out_specs=out_specs,
        out_shape=out_shape,
        scratch_shapes=[pltpu.VMEM((nj, K, tn), BF)],
        compiler_params=_params("arbitrary", "arbitrary"),
        name=name,
    )(*args)
    return res if cast is not None else res[0]


def _gu_kernel(x_ref, wg_ref, wu_ref, cast_ref, o_ref, cast_o_ref, wgb_ref, wub_ref):
    @pl.when(pl.program_id(1) == 0)
    def _():
        wgb_ref[...] = wg_ref[...].astype(BF)
        wub_ref[...] = wu_ref[...].astype(BF)

    x = x_ref[...]
    g = jnp.dot(x, wgb_ref[...], preferred_element_type=F32)
    u = jnp.dot(x, wub_ref[...], preferred_element_type=F32)
    o_ref[...] = (g * (1.0 / (1.0 + jnp.exp(-g))) * u).astype(o_ref.dtype)
    cast_o_ref[...] = cast_ref[...].astype(cast_o_ref.dtype)


def _cast_rider(src, layer, n_steps, step_of):
    R, C = src.shape[1], src.shape[2]
    rows = R // n_steps
    assert R % n_steps == 0 and rows % 16 == 0
    return (pl.BlockSpec((None, rows, C), lambda *ids: (layer, step_of(*ids), 0)),
            pl.BlockSpec((rows, C), lambda *ids: (step_of(*ids), 0)),
            jax.ShapeDtypeStruct((R, C), BF))


def ffn_gate_up(x, w, layer, w_down, tn=512, tm=1024):
    M, K = x.shape
    nj, ni = D_FF // tn, M // tm
    cast_in, cast_out, cast_shape = _cast_rider(w_down, layer, nj * ni, lambda j, i: j * ni + i)
    return pl.pallas_call(
        _gu_kernel,
        grid=(nj, ni),
        in_specs=[pl.BlockSpec((tm, K), lambda j, i: (i, 0)),
                  pl.BlockSpec((None, K, tn), lambda j, i: (layer, 0, j)),
                  pl.BlockSpec((None, K, tn), lambda j, i: (layer, 0, j + nj)),
                  cast_in],
        out_specs=[pl.BlockSpec((tm, tn), lambda j, i: (i, j)), cast_out],
        out_shape=[jax.ShapeDtypeStruct((M, D_FF), BF), cast_shape],
        scratch_shapes=[pltpu.VMEM((K, tn), BF), pltpu.VMEM((K, tn), BF)],
        compiler_params=_params("arbitrary", "arbitrary"),
        name="ffn_gate_up",
    )(x, w, w, w_down)


def _proj_res_kernel(has_x2, n_next, n_sub, *refs):
    refs = list(refs)
    x1_ref = refs.pop(0)
    x2_ref = refs.pop(0) if has_x2 else None
    w_ref, xres_ref, g_ref = refs.pop(0), refs.pop(0), refs.pop(0)
    gn_ref = refs.pop(0) if n_next else None
    xo_ref = refs.pop(0)
    uo_refs = [refs.pop(0) for _ in range(n_next)]
    k1 = x1_ref.shape[1]
    sub = xo_ref.shape[0] // n_sub
    for h in range(n_sub):
        rows = slice(h * sub, (h + 1) * sub)
        acc = jnp.dot(x1_ref[rows, :], w_ref[:k1, :], preferred_element_type=F32)
        if has_x2:
            acc = acc + jnp.dot(x2_ref[rows, :], w_ref[k1:, :], preferred_element_type=F32)
        xn = xres_ref[rows, :] + _rms(acc, g_ref[...])
        xo_ref[rows, :] = xn
        if n_next:
            y = xn * lax.rsqrt(jnp.mean(xn * xn, axis=-1, keepdims=True) + EPS)
            for t, uo_ref in enumerate(uo_refs):
                uo_ref[rows, :] = (y * gn_ref[t:t + 1, :]).astype(uo_ref.dtype)


def proj_residual_norm(x1, x2, x2_col0, k2, w, layer, xres, g, gn, tm=512, n_sub=2):
    M, K1 = x1.shape
    K, N = w.shape[1], w.shape[2]
    has_x2 = x2 is not None
    n_next = 0 if gn is None else gn.shape[0]
    assert K == K1 + (k2 if has_x2 else 0)
    row = lambda i: (i, 0)
    in_specs = [pl.BlockSpec((tm, K1), row)]
    args = [x1]
    if has_x2:
        assert x2_col0 % k2 == 0
        in_specs.append(pl.BlockSpec((tm, k2), lambda i: (i, x2_col0 // k2)))
        args.append(x2)
    in_specs += [pl.BlockSpec((None, K, N), lambda i: (layer, 0, 0), pipeline_mode=pl.Buffered(1)),
                 pl.BlockSpec((tm, N), row),
                 pl.BlockSpec((1, N), lambda i: (0, 0))]
    args += [w, xres, g.reshape(1, N)]
    out_specs = [pl.BlockSpec((tm, N), row)]
    out_shape = [jax.ShapeDtypeStruct((M, N), F32)]
    if n_next:
        in_specs.append(pl.BlockSpec((n_next, N), lambda i: (0, 0)))
        args.append(gn)
        out_specs += [pl.BlockSpec((tm, N), row)] * n_next
        out_shape += [jax.ShapeDtypeStruct((M, N), BF)] * n_next
    res = pl.pallas_call(
        functools.partial(_proj_res_kernel, has_x2, n_next, n_sub),
        grid=(M // tm,),
        in_specs=in_specs,
        out_specs=out_specs,
        out_shape=out_shape,
        compiler_params=_params("arbitrary"),
        name="proj_residual_norm",
    )(*args)
    return res[0], list(res[1:])


def _divmod_pow2(n, d):
    assert d & (d - 1) == 0
    return lax.shift_right_logical(n, d.bit_length() - 1), n & (d - 1)


def _dil_kernel(*refs):
    ng = len(DIL_GROUPS)
    q_refs, k_refs, v_refs = refs[:ng], refs[ng:2 * ng], refs[2 * ng:3 * ng]
    o_ref, o_scr, l_scr, stage_scr, ostage_scr = refs[3 * ng:]
    S = o_ref.shape[1]
    blk = A_BLOCK
    n_heads = ng * A_HEADS_PER_GROUP
    j = pl.program_id(1)
    for gi, (w, d) in enumerate(DIL_GROUPS):
        q_ref, k_ref, v_ref = q_refs[gi], k_refs[gi], v_refs[gi]
        nqb = S // d // blk
        max_dist = w // d
        has_prev = nqb > 1
        nk = 2 * blk if has_prev else blk
        koff = blk if has_prev else 0
        head = (j + (gi * A_HEADS_PER_GROUP + 1)).astype(F32)
        slope = jnp.exp2(jnp.zeros((1, 1), F32) + head * (-8.0 / n_heads))
        kidx = _iota((blk, nk), 1)
        dist = _iota((blk, nk), 0) + koff - kidx
        band = (dist >= 0) & (dist <= max_dist)
        bias = jnp.where(band, (kidx - koff).astype(F32) * (slope * (d * LOG2E)), NEG)
        bias_first = jnp.where(kidx >= koff, bias, NEG)
        row_term = _iota((blk, 1), 0).astype(F32) * (slope * (-float(d)))

        staged = d > STAGE_STRIDE
        if staged:
            assert d % STAGE_STRIDE == 0 and d // STAGE_STRIDE <= STAGE_STRIDE
            quarter = S // STAGE_STRIDE
            for t_idx, ref in enumerate((q_ref, k_ref, v_ref)):
                for r1 in range(STAGE_STRIDE):
                    stage_scr[t_idx, r1 * quarter:(r1 + 1) * quarter, :] = ref[0, pl.ds(r1, quarter, stride=STAGE_STRIDE), :]

        def out_rows(r, i, d=d):
            start = r + i * (d * blk)
            return pl.ds(start, blk, stride=d) if d > 1 else pl.ds(start, blk)

        def staged_rows(r, i, d=d):
            s2 = d // STAGE_STRIDE
            r2, r1 = _divmod_pow2(r, STAGE_STRIDE)
            return pl.ds(r1 * (S // STAGE_STRIDE) + r2 + i * (s2 * blk), blk, stride=s2)

        def load(t_idx, ref, r, i, staged=staged):
            return stage_scr[t_idx, staged_rows(r, i), :] if staged else ref[0, out_rows(r, i), :]

        def body(n0, carry, gi=gi, d=d, has_prev=has_prev, q_ref=q_ref, k_ref=k_ref, v_ref=v_ref,
                 bias=bias, bias_first=bias_first, row_term=row_term, out_rows=out_rows, load=load,
                 staged=staged, staged_rows=staged_rows):
            ri = [_divmod_pow2(n0 * DIL_GROUP_BLOCKS + g, d)[::-1] for g in range(DIL_GROUP_BLOCKS)]
            qs = [load(0, q_ref, r, i).astype(BF) for r, i in ri]
            if has_prev:
                ips = [jnp.maximum(i - 1, 0) for _, i in ri]
                ks = [jnp.concatenate([load(1, k_ref, r, ip), load(1, k_ref, r, i)], axis=0).astype(BF)
                      for (r, i), ip in zip(ri, ips)]
                vs = [jnp.concatenate([load(2, v_ref, r, ip), load(2, v_ref, r, i)], axis=0).astype(BF)
                      for (r, i), ip in zip(ri, ips)]
                bs = [jnp.where(i > 0, bias, bias_first) for _, i in ri]
            else:
                ks = [load(1, k_ref, r, i).astype(BF) for r, i in ri]
                vs = [load(2, v_ref, r, i).astype(BF) for r, i in ri]
                bs = [bias] * len(ri)
            ss = [lax.dot_general(q, k, NT, preferred_element_type=F32) + b for q, k, b in zip(qs, ks, bs)]
            ms = [jnp.max(s, axis=-1, keepdims=True) for s in ss]
            ps = [jnp.exp2(s - m) for s, m in zip(ss, ms)]
            ls = [jnp.sum(p, axis=-1, keepdims=True) for p in ps]
            os_ = [jnp.dot(p.astype(BF), v, preferred_element_type=F32) * (1.0 / l) for p, v, l in zip(ps, vs, ls)]
            for (r, i), o, m, l in zip(ri, os_, ms, ls):
                lse_b = jnp.broadcast_to((m + jnp.log2(l)) * LN2 + row_term, (blk, HEAD_DIM))
                if staged:
                    rows = staged_rows(r, i)
                    ostage_scr[0, rows, :] = o
                    ostage_scr[1, rows, :] = lse_b
                else:
                    rows = out_rows(r, i)
                    o_scr[gi, rows, :] = o
                    l_scr[gi, rows, :] = lse_b
            return carry

        assert (d * nqb) % DIL_GROUP_BLOCKS == 0
        lax.fori_loop(0, d * nqb // DIL_GROUP_BLOCKS, body, 0)
        if staged:
            for r1 in range(STAGE_STRIDE):
                dst = pl.ds(r1, quarter, stride=STAGE_STRIDE)
                o_scr[gi, dst, :] = ostage_scr[0, r1 * quarter:(r1 + 1) * quarter, :]
                l_scr[gi, dst, :] = ostage_scr[1, r1 * quarter:(r1 + 1) * quarter, :]

    tr = 256
    for c in range(S // tr):
        rows = slice(c * tr, (c + 1) * tr)
        ls = [l_scr[gi, rows, :] for gi in range(ng)]
        m = functools.reduce(jnp.maximum, ls)
        es = [jnp.exp(x - m) for x in ls]
        num = sum(e * o_scr[gi, rows, :] for gi, e in enumerate(es))
        o_ref[0, rows, :] = (num * (1.0 / sum(es))).astype(o_ref.dtype)


def dilated_attention(q, k, v):
    B, S, _ = q.shape
    ng, Hg = len(DIL_GROUPS), A_HEADS_PER_GROUP
    for w, d in DIL_GROUPS:
        assert w // d <= A_BLOCK and S % (d * A_BLOCK) == 0
    specs = [pl.BlockSpec((1, S, HEAD_DIM), lambda b, j, gi=gi: (b, 0, gi * Hg + j)) for gi in range(ng)]
    return pl.pallas_call(
        _dil_kernel,
        grid=(B, Hg),
        in_specs=specs * 3,
        out_specs=pl.BlockSpec((1, S, HEAD_DIM), lambda b, j: (b, 0, j)),
        out_shape=jax.ShapeDtypeStruct((B, S, Hg * HEAD_DIM), BF),
        scratch_shapes=[pltpu.VMEM((ng, S, HEAD_DIM), F32), pltpu.VMEM((ng, S, HEAD_DIM), F32),
                        pltpu.VMEM((3, S, HEAD_DIM), F32), pltpu.VMEM((2, S, HEAD_DIM), F32)],
        compiler_params=_params("arbitrary", "arbitrary"),
        name="dilated_attention",
    )(*([q] * ng + [k] * ng + [v] * ng))


def _mem_attn_kernel(q_ref, kv_ref, o_ref):
    for h in range(MEM_HEADS):
        c = slice(h * HEAD_DIM, (h + 1) * HEAD_DIM)
        k = kv_ref[0, :, c]
        v = kv_ref[0, :, MEM_HEADS * HEAD_DIM + h * HEAD_DIM:MEM_HEADS * HEAD_DIM + (h + 1) * HEAD_DIM]
        s = lax.dot_general(q_ref[0, :, c], k, NT, preferred_element_type=F32) * SCALE
        m = jnp.max(s, axis=-1, keepdims=True)
        e = jnp.exp(s - m)
        p = e * (1.0 / jnp.sum(e, axis=-1, keepdims=True))
        o_ref[0, :, c] = jnp.dot(p.astype(BF), v, preferred_element_type=F32).astype(o_ref.dtype)


def memory_attention(q, q_col0, kv, layer, tq=1024):
    B, S, _ = q.shape
    w = MEM_HEADS * HEAD_DIM
    cb = q_col0 // w
    return pl.pallas_call(
        _mem_attn_kernel,
        grid=(B, S // tq),
        in_specs=[pl.BlockSpec((1, tq, w), lambda b, i: (b, i, cb)),
                  pl.BlockSpec((None, 1, N_MEM, 2 * w), lambda b, i: (layer, b, 0, 0))],
        out_specs=pl.BlockSpec((1, tq, w), lambda b, i: (b, i, 0)),
        out_shape=jax.ShapeDtypeStruct((B, S, w), BF),
        compiler_params=_params("arbitrary", "arbitrary"),
        name="memory_attention",
    )(q, kv)


def _gelu_tanh(x):
    return 0.5 * x * (1.0 + jnp.tanh(math.sqrt(2.0 / math.pi) * (x + 0.044715 * (x * x * x))))


def _compress_kernel(out_scale, x_ref, w1_ref, w2_ref, pe_ref, o_ref, w1b_ref):
    @pl.when((pl.program_id(0) == 0) & (pl.program_id(1) == 0))
    def _():
        w1b_ref[...] = w1_ref[...].astype(BF)

    nchunk = o_ref.shape[1]
    top = jnp.zeros((nchunk, CMP_HIDDEN), F32)
    bot = jnp.zeros((nchunk, CMP_HIDDEN), F32)
    for l in range(CMP_STRIDE):
        xl = x_ref[0, pl.ds(l, nchunk, stride=CMP_STRIDE), :]
        lo, hi = l, CMP_STRIDE + l
        top += jnp.dot((xl + pe_ref[lo:lo + 1, :]).astype(BF),
                       w1b_ref[lo * HEAD_DIM:(lo + 1) * HEAD_DIM, :], preferred_element_type=F32)
        bot += jnp.dot((xl + pe_ref[hi:hi + 1, :]).astype(BF),
                       w1b_ref[hi * HEAD_DIM:(hi + 1) * HEAD_DIM, :], preferred_element_type=F32)
    hid = _gelu_tanh(top + pltpu.roll(bot, nchunk - 1, 0))
    out = jnp.dot(hid.astype(BF), w2_ref[...].astype(BF), preferred_element_type=F32)
    o_ref[0] = (out * out_scale).astype(o_ref.dtype)


def compress_blocks(kvc, sec, w1, w2, pe, out_scale=1.0):
    B, S, C = kvc.shape
    assert CMP_LEN == 2 * CMP_STRIDE
    G = B_KV_GROUPS
    nchunk = S // CMP_STRIDE
    return pl.pallas_call(
        functools.partial(_compress_kernel, out_scale),
        grid=(B, G),
        in_specs=[pl.BlockSpec((1, S, HEAD_DIM), lambda b, g: (b, 0, sec * G + g)),
                  pl.BlockSpec(w1.shape, lambda b, g: (0, 0)),
                  pl.BlockSpec(w2.shape, lambda b, g: (0, 0)),
                  pl.BlockSpec(pe.shape, lambda b, g: (0, 0))],
        out_specs=pl.BlockSpec((1, nchunk, HEAD_DIM), lambda b, g: (b, 0, g)),
        out_shape=jax.ShapeDtypeStruct((B, nchunk, G * HEAD_DIM), BF),
        scratch_shapes=[pltpu.VMEM(w1.shape, BF)],
        compiler_params=_params("arbitrary", "arbitrary"),
        name=f"compress_{sec}",
    )(kvc, w1, w2, pe)


ONES_ROWS = 16
AUG_K = LANES
NEG_ROWS = 16


def _key_aug(nk, bps):
    idx = _iota((nk, AUG_K), 0)
    lane = _iota((nk, AUG_K), 1)
    piece = jnp.where((lane - NEG_ROWS) % 2 == 0, (idx // 256) * 256, idx % 256)
    val = jnp.where(lane < bps, (idx // SLC_LEN == lane).astype(jnp.int32),
                    jnp.where((lane >= NEG_ROWS) & (lane < NEG_ROWS + 6), piece, 0))
    return val.astype(F32).astype(BF)


def _offset_masks(n, nk, tq, lo, hi):
    dist = (_iota((n, nk, tq), 0) * tq + _iota((n, nk, tq), 2)) - _iota((n, nk, tq), 1)
    return jnp.where((dist >= lo) & (dist <= hi), 0.0, NEG).astype(F32)


def _t_bf(x):
    return x.astype(F32).T.astype(BF)


def _nsa_kernel(tq, q_ref, gate_ref, kc_ref, vc_ref, ks_ref, vs_ref, kw_ref, vw_ref,
                sl_ref, kaug_w_ref, kaug_s_ref, wmask_ref, cmask_ref, o_ref,
                vst_ref, vwt_ref, qt_ref, ocmp_ref, neg_ref, raw_a_ref, raw_b_ref, pc_ref):
    S = ks_ref.shape[1]
    ncmp = kc_ref.shape[1]
    nsel = S // SLC_LEN
    i = pl.program_id(2)

    t0 = i * tq
    tw = B_REP * tq
    nt = S // tq
    n_back = -(-(WIN_LEN - 1) // tq)
    nwin = (n_back + 1) * tq
    kcs = vst_ref.shape[2]
    bps = kcs // SLC_LEN

    def per_head(fn, n=B_REP):
        return jnp.concatenate([fn(r) for r in range(n)], axis=1)

    def bf_exact(x):
        return x.astype(BF).astype(F32)

    @pl.when(i == 0)
    def _():
        for src_ref, dst_ref in ((vs_ref, vst_ref), (vw_ref, vwt_ref)):
            n, _, w = dst_ref.shape
            ones = jnp.ones((ONES_ROWS, w), BF)
            for c in range(n):
                dst_ref[c] = jnp.concatenate([_t_bf(src_ref[0, c * w:(c + 1) * w, :]), ones], axis=0)
        vct = _t_bf(vc_ref[0])
        for t in range(nt):
            for r in range(B_REP):
                qt_ref[t, r] = _t_bf(q_ref[0, t * tq:(t + 1) * tq, r * HEAD_DIM:(r + 1) * HEAD_DIM])
        q_w = jnp.concatenate([qt_ref[t, r] for r in range(B_REP) for t in range(nt)], axis=1)
        sl2_w = per_head(lambda r: jnp.broadcast_to(sl_ref[0, 0:1, r:r + 1], (1, S))) * LOG2E
        pos = _iota((1, S), 1)
        pos_w = per_head(lambda r: pos)

        dist_c = pos_w - (_iota((ncmp, 1), 0) * CMP_STRIDE + (CMP_LEN - 1))
        s = (jnp.dot(kc_ref[0], q_w, preferred_element_type=F32)
             + jnp.where(dist_c >= 0, dist_c.astype(F32) * (-sl2_w), NEG))
        e = jnp.exp2(s - jnp.max(s, axis=0, keepdims=True))
        has_block = (pos_w >= CMP_LEN - 1).astype(F32)
        p = e * (has_block / jnp.maximum(jnp.sum(e, axis=0, keepdims=True), 1e-30))
        o_cmp_w = jnp.dot(vct, p.astype(BF), preferred_element_type=F32)
        for t in range(nt):
            for r in range(B_REP):
                ocmp_ref[t, r] = o_cmp_w[:, r * S + t * tq:r * S + (t + 1) * tq]

        ratio = SLC_LEN // CMP_STRIDE
        assert SLC_LEN % CMP_STRIDE == 0 and CMP_LEN <= SLC_LEN and ncmp >= nsel * ratio
        p_heads = sum(p[:, r * S:(r + 1) * S] for r in range(B_REP))
        for lt in range(S // LANES):
            pc_ref[lt] = p_heads[:, lt * LANES:(lt + 1) * LANES]
        imp = jnp.zeros((nsel, S), F32)
        for o in range(1 - (-(-CMP_LEN // CMP_STRIDE)), ratio):
            overlap = max(min(o * CMP_STRIDE + CMP_LEN, SLC_LEN) - max(o * CMP_STRIDE, 0), 0)
            if overlap == 0:
                continue
            part = jnp.concatenate([pc_ref[lt, pl.ds(o % ratio, nsel, stride=ratio), :]
                                    for lt in range(S // LANES)], axis=1)
            if o < 0:
                part = jnp.where(_iota((nsel, S), 0) == 0, 0.0, pltpu.roll(part, 1, 0))
            imp = imp + (overlap / CMP_LEN) * part

        jb = _iota((nsel, S), 0)
        cur = pos // SLC_LEN
        forced = (jb == 0) | (jb == cur) | (jb == cur - 1)
        imp = jnp.where(forced, SEL_FORCE, jnp.where(jb > cur, -SEL_FORCE, imp))
        beaten_by = jnp.zeros((nsel, S), jnp.int32)
        for j in range(nsel):
            row = imp[j:j + 1, :]
            beats = (row > imp) | ((row == imp) & (jb > j))
            beaten_by = beaten_by + beats.astype(jnp.int32)
        neg_w = jnp.where(beaten_by >= min(SLC_TOPK, nsel), NEG, 0.0)
        for t in range(nt):
            neg_ref[t] = neg_w[:, t * tq:(t + 1) * tq]

    q_all = per_head(lambda r: qt_ref[i, r])
    o_cmp = per_head(lambda r: ocmp_ref[i, r])
    sl_row = per_head(lambda r: jnp.broadcast_to(sl_ref[0, 0:1, r:r + 1], (1, tq)))
    sl2_row = sl_row * LOG2E

    c1 = bf_exact(sl2_row)
    c2 = bf_exact(sl2_row - c1)
    c3 = bf_exact(sl2_row - c1 - c2)
    slope_rows = jnp.concatenate([c1, c1, c2, c2, c3, c3, jnp.zeros((AUG_K - NEG_ROWS - 6, tw), F32)],
                                 axis=0).astype(BF)

    def scores(k_c, k_aug, neg_rows):
        q_full = jnp.concatenate([q_all, neg_rows, slope_rows], axis=0)
        return jnp.dot(jnp.concatenate([k_c, k_aug], axis=1), q_full, preferred_element_type=F32)

    def neg_rows_of(n):
        rows = neg_ref[i, pl.ds(pl.multiple_of(n * bps, bps), bps), :]
        return jnp.concatenate([per_head(lambda r: rows), jnp.zeros((NEG_ROWS - bps, tw), F32)], axis=0).astype(BF)

    no_rows = jnp.zeros((NEG_ROWS, tw), BF)
    c0 = jnp.maximum(i - n_back, 0)
    wbase = pl.multiple_of(c0 * tq, tq)
    raw_w = scores(kw_ref[0, pl.ds(wbase, nwin), :], kaug_w_ref[...], no_rows)
    k_aug = kaug_s_ref[...]
    n_last = (t0 + tq - 1) // kcs
    raw_d = scores(ks_ref[0, pl.ds(pl.multiple_of(n_last * kcs, kcs), kcs), :], k_aug, no_rows)

    def softmax_step(raw, vt_c, mask, c_off, st):
        m, acc = st
        s = raw if mask is None else raw + per_head(lambda r: mask)
        m_new = jnp.maximum(m, jnp.max(s, axis=0, keepdims=True) + c_off)
        p = jnp.exp2(s - (m_new - c_off))
        acc = jnp.exp2(m - m_new) * acc + jnp.dot(vt_c, p.astype(BF), preferred_element_type=F32)
        return m_new, acc

    def finish(st):
        _, acc = st
        return acc[:HEAD_DIM] * (1.0 / acc[HEAD_DIM:HEAD_DIM + 1])

    init = (jnp.full((1, tw), NEG, F32), jnp.zeros((HEAD_DIM + ONES_ROWS, tw), F32))
    no_off = jnp.zeros((1, 1), F32)

    vwt = jnp.concatenate([vwt_ref[c0 + j] for j in range(n_back + 1)], axis=1)
    o_win = finish(softmax_step(raw_w, vwt, wmask_ref[i - c0], no_off, init))

    def slab_scores(n):
        return scores(ks_ref[0, pl.ds(pl.multiple_of(n * kcs, kcs), kcs), :], k_aug, neg_rows_of(n))

    raw_a_ref[...] = slab_scores(0)
    sel_rows = [jnp.broadcast_to(neg_ref[i, pl.ds(n_last * bps + b, 1), :], (SLC_LEN, tq)) for b in range(bps)]
    mask_d = jnp.concatenate(sel_rows, axis=0) + cmask_ref[i - n_last * (kcs // tq)]
    st = softmax_step(raw_d, vst_ref[n_last], mask_d, (n_last * kcs - t0).astype(F32) * sl2_row, init)

    def sel_slab(n, st, src_ref, dst_ref):
        dst_ref[...] = slab_scores(jnp.minimum(n + 1, n_last))
        return softmax_step(src_ref[...], vst_ref[n], None, (n * kcs - t0).astype(F32) * sl2_row, st)

    def sel_pair(j, st):
        st = sel_slab(2 * j, st, raw_a_ref, raw_b_ref)
        return lax.cond(2 * j + 1 < n_last, lambda s: sel_slab(2 * j + 1, s, raw_b_ref, raw_a_ref),
                        lambda s: s, st)

    o_slc = finish(lax.fori_loop(0, (n_last + 1) // 2, sel_pair, st))

    gate_t = (1.0 / (1.0 + jnp.exp(-gate_ref[0]))).T
    for r in range(B_REP):
        c = slice(r * tq, (r + 1) * tq)
        o_t = (gate_t[3 * r:3 * r + 1] * o_cmp[:, c] + gate_t[3 * r + 1:3 * r + 2] * o_slc[:, c]
               + gate_t[3 * r + 2:3 * r + 3] * o_win[:, c])
        o_ref[0, :, r * HEAD_DIM:(r + 1) * HEAD_DIM] = o_t.T.astype(o_ref.dtype)


def nsa_attention(qp, gates, kc, vc, kv, tq=256):
    B, S, _ = qp.shape
    G = B_KV_GROUPS
    gq = B_REP * HEAD_DIM
    ncmp = kc.shape[1]
    kcs = 2 * tq
    nsel = S // SLC_LEN
    assert kcs % SLC_LEN == 0 and S % kcs == 0 and S >= (-(-(WIN_LEN - 1) // tq) + 1) * tq
    slopes = 2.0 ** (-8.0 * jnp.arange(1, B_HEADS + 1, dtype=F32) / B_HEADS)
    sl = jnp.zeros((G, 8, LANES), F32).at[:, :, :B_REP].set(
        jnp.broadcast_to(slopes.reshape(G, 1, B_REP), (G, 8, B_REP)))
    kvspec = lambda sec: pl.BlockSpec((1, S, HEAD_DIM), lambda b, g, i: (b, 0, sec * G + g))
    cspec = pl.BlockSpec((1, ncmp, HEAD_DIM), lambda b, g, i: (b, 0, g))
    vt_rows = HEAD_DIM + ONES_ROWS
    n_back = -(-(WIN_LEN - 1) // tq)
    nwin = (n_back + 1) * tq
    bps = kcs // SLC_LEN
    assert bps <= NEG_ROWS and NEG_ROWS + 6 <= AUG_K
    consts = [_key_aug(nwin, bps), _key_aug(kcs, bps), _offset_masks(n_back + 1, nwin, tq, 0, WIN_LEN - 1),
              _offset_masks(kcs // tq, kcs, tq, 0, S)]
    whole = lambda a: pl.BlockSpec(a.shape, lambda b, g, i, nd=a.ndim: (0,) * nd)
    return pl.pallas_call(
        functools.partial(_nsa_kernel, tq),
        grid=(B, G, S // tq),
        in_specs=[pl.BlockSpec((1, S, gq), lambda b, g, i: (b, 0, g)),
                  pl.BlockSpec((1, tq, LANES), lambda b, g, i: (b, i, g)),
                  cspec, cspec, kvspec(0), kvspec(1), kvspec(2), kvspec(3),
                  pl.BlockSpec((1, 8, LANES), lambda b, g, i: (g, 0, 0))] + [whole(a) for a in consts],
        out_specs=pl.BlockSpec((1, tq, gq), lambda b, g, i: (b, i, g)),
        out_shape=jax.ShapeDtypeStruct((B, S, B_HEADS * HEAD_DIM), BF),
        scratch_shapes=[pltpu.VMEM((S // kcs, vt_rows, kcs), BF), pltpu.VMEM((S // tq, vt_rows, tq), BF),
                        pltpu.VMEM((S // tq, B_REP, HEAD_DIM, tq), BF), pltpu.VMEM((S // tq, B_REP, HEAD_DIM, tq), F32),
                        pltpu.VMEM((S // tq, nsel, tq), F32),
                        pltpu.VMEM((kcs, B_REP * tq), F32), pltpu.VMEM((kcs, B_REP * tq), F32),
                        pltpu.VMEM((S // LANES, ncmp, LANES), F32)],
        compiler_params=_params("arbitrary", "arbitrary", "arbitrary"),
        name="nsa_attention",
    )(qp, gates, kc, vc, kv, kv, kv, kv, sl, *consts)


def kernel(x, mem, norm_g, a_w_in, a_w_out, b_w_in, b_w_out, mem_w_kv, ffn_w_gu, ffn_w_down,
           kv_norm_g, kv_w, cmp_pe, cmp_wk1, cmp_wk2, cmp_wv1, cmp_wv2):
    B, S, D = x.shape
    M = B * S
    n_a = DEPTH // 2
    qa_cols = len(DIL_GROUPS) * A_HEADS_PER_GROUP * HEAD_DIM
    qb_cols = B_HEADS * HEAD_DIM
    mq_cols = MEM_HEADS * HEAD_DIM
    n_gate = 3 * B_HEADS
    G = B_KV_GROUPS

    b_w_mq = b_w_in[:, :, qb_cols + n_gate:]
    b_w_gate = jnp.pad(b_w_in[:, :, qb_cols:qb_cols + n_gate].reshape(-1, D, G, n_gate // G),
                       ((0, 0), (0, 0), (0, 0), (0, LANES - n_gate // G))).reshape(-1, D, G * LANES)
    kv_w3 = kv_w[None]

    x2 = x.reshape(M, D)
    mkv = memory_kv_all_layers(mem.reshape(B * N_MEM, D), norm_g[:, 4], mem_w_kv).reshape(
        DEPTH, B, N_MEM, 2 * mq_cols)
    u = rmsnorm_rows(x2, norm_g[0, 0])
    shared = None
    for l in range(DEPTH):
        g = norm_g[l]
        if l < n_a:
            q, w_out = matmul(u, a_w_in, l, col0=0, ncols=qa_cols, out_dtype=F32, cast=(a_w_out, l), name="a_q")
            k = matmul(u, a_w_in, l, col0=qa_cols, ncols=qa_cols, out_dtype=F32,
                       col_scale=jnp.full((qa_cols,), SCALE2, F32), name="a_k")
            v = matmul(u, a_w_in, l, col0=2 * qa_cols, ncols=qa_cols, out_dtype=F32, name="a_v")
            mq = matmul(u, a_w_in, l, col0=3 * qa_cols, ncols=mq_cols, name="a_mq").reshape(B, S, mq_cols)
            o_main = dilated_attention(*(t.reshape(B, S, qa_cols) for t in (q, k, v)))
            o_mem = memory_attention(mq, 0, mkv, l)
        else:
            li = l - n_a
            if shared is None:
                kvc = matmul(kvn, kv_w3, 0, col0=0, ncols=2 * G * HEAD_DIM, out_dtype=F32,
                             name="kv_cmp").reshape(B, S, 2 * G * HEAD_DIM)
                sec = G * HEAD_DIM
                key_scale = jnp.tile(jnp.concatenate([jnp.full((sec,), SCALE2, F32), jnp.ones((sec,), F32)]), 2)
                kvr = matmul(kvn, kv_w3, 0, col0=2 * G * HEAD_DIM, ncols=4 * G * HEAD_DIM, col_scale=key_scale,
                             name="kv_rest").reshape(B, S, 4 * G * HEAD_DIM)
                kc = compress_blocks(kvc, 0, cmp_wk1, cmp_wk2, cmp_pe[0], out_scale=SCALE2)
                vc = compress_blocks(kvc, 1, cmp_wv1, cmp_wv2, cmp_pe[1])
                shared = (kc, vc, kvr)
            qb = matmul(u, b_w_in, li, col0=0, ncols=qb_cols, name="b_q").reshape(B, S, qb_cols)
            mq = matmul(u, b_w_mq, li, name="b_mq").reshape(B, S, mq_cols)
            gates, w_out = matmul(u, b_w_gate, li, out_dtype=F32, cast=(b_w_out, li), name="b_gate")
            o_main = nsa_attention(qb, gates.reshape(B, S, G * LANES), *shared)
            o_mem = memory_attention(mq, 0, mkv, l)
        x2, (u,) = proj_residual_norm(o_main.reshape(M, -1), o_mem.reshape(M, mq_cols), 0, mq_cols,
                                      w_out[None], 0, x2, g[1], g[2:3])
        h, w_down = ffn_gate_up(u, ffn_w_gu, l, ffn_w_down)
        next_gains = [norm_g[l + 1, 0]] if l + 1 < DEPTH else []
        if l + 1 == n_a:
            next_gains.append(kv_norm_g)
        x2, us = proj_residual_norm(h, None, 0, 0, w_down[None], 0, x2, g[3],
                                    jnp.stack(next_gains) if next_gains else None, tm=256, n_sub=2)
        u = us[0] if us else None
        if l + 1 == n_a:
            kvn = us[1]
    return x2.reshape(B, S, D)
```

```python
import functools
import math

import jax
import jax.numpy as jnp
from jax import lax
from jax.experimental import pallas as pl
from jax.experimental.pallas import tpu as pltpu

BF = jnp.bfloat16
F32 = jnp.float32

D_MODEL = 2048
DEPTH = 4
HEAD_DIM = 128
N_MEM = 256
MEM_HEADS = 4
DIL_GROUPS = ((128, 1), (512, 4), (2048, 16))
A_HEADS_PER_GROUP = 8
A_BLOCK = 128
B_HEADS = 12
B_KV_GROUPS = 4
B_REP = B_HEADS // B_KV_GROUPS
CMP_LEN = 32
CMP_STRIDE = 16
CMP_HIDDEN = 512
SLC_LEN = 64
SLC_TOPK = 16
SEL_FORCE = 1e4
WIN_LEN = 512
D_FF = 5632
EPS = 1e-6

LANES = 128
VMEM_LIMIT = 56 * 1024 * 1024
NEG = -1e30
SCALE = 1.0 / math.sqrt(HEAD_DIM)
LOG2E = math.log2(math.e)
LN2 = math.log(2.0)
SCALE2 = SCALE * LOG2E
NT = (((1,), (1,)), ((), ()))
DIL_GROUP_BLOCKS = 16
STAGE_STRIDE = 4


def _params(*sem):
    return pltpu.CompilerParams(dimension_semantics=sem, vmem_limit_bytes=VMEM_LIMIT)


def _rms(x, g):
    return x * lax.rsqrt(jnp.mean(x * x, axis=-1, keepdims=True) + EPS) * g


def _iota(shape, dim):
    return lax.broadcasted_iota(jnp.int32, shape, dim)


def _rmsnorm_kernel(x_ref, g_ref, o_ref):
    o_ref[...] = _rms(x_ref[...], g_ref[...]).astype(o_ref.dtype)


def rmsnorm_rows(x, g, tm=1024):
    M, D = x.shape
    tm = min(tm, M)
    return pl.pallas_call(
        _rmsnorm_kernel,
        grid=(M // tm,),
        in_specs=[pl.BlockSpec((tm, D), lambda i: (i, 0)),
                  pl.BlockSpec((1, D), lambda i: (0, 0))],
        out_specs=pl.BlockSpec((tm, D), lambda i: (i, 0)),
        out_shape=jax.ShapeDtypeStruct((M, D), BF),
        compiler_params=_params("arbitrary"),
        name="rmsnorm",
    )(x, g.reshape(1, D))


def _mem_kv_kernel(x_ref, g_ref, w_ref, o_ref):
    xn = _rms(x_ref[...], g_ref[...]).astype(BF)
    o_ref[...] = jnp.dot(xn, w_ref[...].astype(BF), preferred_element_type=F32).astype(o_ref.dtype)


def memory_kv_all_layers(mem2, g, w, tn=512):
    Mm, D = mem2.shape
    L, _, N = w.shape
    return pl.pallas_call(
        _mem_kv_kernel,
        grid=(L, N // tn),
        in_specs=[pl.BlockSpec((Mm, D), lambda l, j: (0, 0)),
                  pl.BlockSpec((None, 1, D), lambda l, j: (l, 0, 0)),
                  pl.BlockSpec((None, D, tn), lambda l, j: (l, 0, j))],
        out_specs=pl.BlockSpec((None, Mm, tn), lambda l, j: (l, 0, j)),
        out_shape=jax.ShapeDtypeStruct((L, Mm, N), BF),
        compiler_params=_params("arbitrary", "arbitrary"),
        name="mem_kv",
    )(mem2, g.reshape(L, 1, D), w)


def _mm_kernel(has_scale, has_cast, *refs):
    refs = list(refs)
    x_ref, w_ref = refs.pop(0), refs.pop(0)
    s_ref = refs.pop(0) if has_scale else None
    cast_ref = refs.pop(0) if has_cast else None
    o_ref = refs.pop(0)
    cast_o_ref = refs.pop(0) if has_cast else None
    wbf_ref = refs.pop(0)
    j = pl.program_id(1)

    @pl.when(pl.program_id(0) == 0)
    def _():
        wbf_ref[j] = w_ref[...].astype(BF)

    acc = jnp.dot(x_ref[...], wbf_ref[j], preferred_element_type=F32)
    if has_scale:
        acc = acc * s_ref[...]
    o_ref[...] = acc.astype(o_ref.dtype)
    if has_cast:
        cast_o_ref[...] = cast_ref[...].astype(cast_o_ref.dtype)


def matmul(x, w, layer, *, col0=0, ncols=None, tm=1024, out_dtype=BF, col_scale=None, cast=None,
           name="matmul"):
    M, K = x.shape
    ncols = w.shape[2] - col0 if ncols is None else ncols
    tn = 1024 if (ncols % 1024 == 0 and col0 % 1024 == 0) else 512
    tm = min(tm, M)
    assert col0 % tn == 0 and ncols % tn == 0 and M % tm == 0
    c0 = col0 // tn
    nj = ncols // tn
    in_specs = [pl.BlockSpec((tm, K), lambda i, j: (i, 0)),
                pl.BlockSpec((None, K, tn), lambda i, j: (layer, 0, jnp.where(i == 0, j, 0) + c0))]
    args = [x, w]
    out_specs = [pl.BlockSpec((tm, tn), lambda i, j: (i, j))]
    out_shape = [jax.ShapeDtypeStruct((M, ncols), out_dtype)]
    if col_scale is not None:
        in_specs.append(pl.BlockSpec((1, tn), lambda i, j: (0, j)))
        args.append(col_scale.reshape(1, ncols))
    if cast is not None:
        cast_in, cast_out, cast_shape = _cast_rider(cast[0], cast[1], (M // tm) * nj, lambda i, j: i * nj + j)
        in_specs.append(cast_in)
        args.append(cast[0])
        out_specs.append(cast_out)
        out_shape.append(cast_shape)
    res = pl.pallas_call(
        functools.partial(_mm_kernel, col_scale is not None, cast is not None),
        grid=(M // tm, nj),
        in_specs=in_specs,
        out_specs=out_specs,
        out_shape=out_shape,
        scratch_shapes=[pltpu.VMEM((nj, K, tn), BF)],
        compiler_params=_params("arbitrary", "arbitrary"),
        name=name,
    )(*args)
    return res if cast is not None else res[0]


def _gu_kernel(x_ref, wg_ref, wu_ref, cast_ref, o_ref, cast_o_ref, wgb_ref, wub_ref):
    @pl.when(pl.program_id(1) == 0)
    def _():
        wgb_ref[...] = wg_ref[...].astype(BF)
        wub_ref[...] = wu_ref[...].astype(BF)

    x = x_ref[...]
    g = jnp.dot(x, wgb_ref[...], preferred_element_type=F32)
    u = jnp.dot(x, wub_ref[...], preferred_element_type=F32)
    o_ref[...] = (g * (1.0 / (1.0 + jnp.exp(-g))) * u).astype(o_ref.dtype)
    cast_o_ref[...] = cast_ref[...].astype(cast_o_ref.dtype)


def _cast_rider(src, layer, n_steps, step_of):
    R, C = src.shape[1], src.shape[2]
    rows = R // n_steps
    assert R % n_steps == 0 and rows % 16 == 0
    return (pl.BlockSpec((None, rows, C), lambda *ids: (layer, step_of(*ids), 0)),
            pl.BlockSpec((rows, C), lambda *ids: (step_of(*ids), 0)),
            jax.ShapeDtypeStruct((R, C), BF))


def ffn_gate_up(x, w, layer, w_down, tn=512, tm=1024):
    M, K = x.shape
    nj, ni = D_FF // tn, M // tm
    cast_in, cast_out, cast_shape = _cast_rider(w_down, layer, nj * ni, lambda j, i: j * ni + i)
    return pl.pallas_call(
        _gu_kernel,
        grid=(nj, ni),
        in_specs=[pl.BlockSpec((tm, K), lambda j, i: (i, 0)),
                  pl.BlockSpec((None, K, tn), lambda j, i: (layer, 0, j)),
                  pl.BlockSpec((None, K, tn), lambda j, i: (layer, 0, j + nj)),
                  cast_in],
        out_specs=[pl.BlockSpec((tm, tn), lambda j, i: (i, j)), cast_out],
        out_shape=[jax.ShapeDtypeStruct((M, D_FF), BF), cast_shape],
        scratch_shapes=[pltpu.VMEM((K, tn), BF), pltpu.VMEM((K, tn), BF)],
        compiler_params=_params("arbitrary", "arbitrary"),
        name="ffn_gate_up",
    )(x, w, w, w_down)


def _proj_res_kernel(has_x2, n_next, n_sub, *refs):
    refs = list(refs)
    x1_ref = refs.pop(0)
    x2_ref = refs.pop(0) if has_x2 else None
    w_ref, xres_ref, g_ref = refs.pop(0), refs.pop(0), refs.pop(0)
    gn_ref = refs.pop(0) if n_next else None
    xo_ref = refs.pop(0)
    uo_refs = [refs.pop(0) for _ in range(n_next)]
    k1 = x1_ref.shape[1]
    sub = xo_ref.shape[0] // n_sub
    for h in range(n_sub):
        rows = slice(h * sub, (h + 1) * sub)
        acc = jnp.dot(x1_ref[rows, :], w_ref[:k1, :], preferred_element_type=F32)
        if has_x2:
            acc = acc + jnp.dot(x2_ref[rows, :], w_ref[k1:, :], preferred_element_type=F32)
        xn = xres_ref[rows, :] + _rms(acc, g_ref[...])
        xo_ref[rows, :] = xn
        if n_next:
            y = xn * lax.rsqrt(jnp.mean(xn * xn, axis=-1, keepdims=True) + EPS)
            for t, uo_ref in enumerate(uo_refs):
                uo_ref[rows, :] = (y * gn_ref[t:t + 1, :]).astype(uo_ref.dtype)


def proj_residual_norm(x1, x2, x2_col0, k2, w, layer, xres, g, gn, tm=512, n_sub=2):
    M, K1 = x1.shape
    K, N = w.shape[1], w.shape[2]
    has_x2 = x2 is not None
    n_next = 0 if gn is None else gn.shape[0]
    assert K == K1 + (k2 if has_x2 else 0)
    row = lambda i: (i, 0)
    in_specs = [pl.BlockSpec((tm, K1), row)]
    args = [x1]
    if has_x2:
        assert x2_col0 % k2 == 0
        in_specs.append(pl.BlockSpec((tm, k2), lambda i: (i, x2_col0 // k2)))
        args.append(x2)
    in_specs += [pl.BlockSpec((None, K, N), lambda i: (layer, 0, 0), pipeline_mode=pl.Buffered(1)),
                 pl.BlockSpec((tm, N), row),
                 pl.BlockSpec((1, N), lambda i: (0, 0))]
    args += [w, xres, g.reshape(1, N)]
    out_specs = [pl.BlockSpec((tm, N), row)]
    out_shape = [jax.ShapeDtypeStruct((M, N), F32)]
    if n_next:
        in_specs.append(pl.BlockSpec((n_next, N), lambda i: (0, 0)))
        args.append(gn)
        out_specs += [pl.BlockSpec((tm, N), row)] * n_next
        out_shape += [jax.ShapeDtypeStruct((M, N), BF)] * n_next
    res = pl.pallas_call(
        functools.partial(_proj_res_kernel, has_x2, n_next, n_sub),
        grid=(M // tm,),
        in_specs=in_specs,
        out_specs=out_specs,
        out_shape=out_shape,
        compiler_params=_params("arbitrary"),
        name="proj_residual_norm",
    )(*args)
    return res[0], list(res[1:])


def _divmod_pow2(n, d):
    assert d & (d - 1) == 0
    return lax.shift_right_logical(n, d.bit_length() - 1), n & (d - 1)


def _dil_kernel(*refs):
    ng = len(DIL_GROUPS)
    q_refs, k_refs, v_refs = refs[:ng], refs[ng:2 * ng], refs[2 * ng:3 * ng]
    o_ref, o_scr, l_scr, stage_scr, ostage_scr = refs[3 * ng:]
    S = o_ref.shape[1]
    blk = A_BLOCK
    n_heads = ng * A_HEADS_PER_GROUP
    j = pl.program_id(1)
    for gi, (w, d) in enumerate(DIL_GROUPS):
        q_ref, k_ref, v_ref = q_refs[gi], k_refs[gi], v_refs[gi]
        nqb = S // d // blk
        max_dist = w // d
        has_prev = nqb > 1
        nk = 2 * blk if has_prev else blk
        koff = blk if has_prev else 0
        head = (j + (gi * A_HEADS_PER_GROUP + 1)).astype(F32)
        slope = jnp.exp2(jnp.zeros((1, 1), F32) + head * (-8.0 / n_heads))
        kidx = _iota((blk, nk), 1)
        dist = _iota((blk, nk), 0) + koff - kidx
        band = (dist >= 0) & (dist <= max_dist)
        bias = jnp.where(band, (kidx - koff).astype(F32) * (slope * (d * LOG2E)), NEG)
        bias_first = jnp.where(kidx >= koff, bias, NEG)
        row_term = _iota((blk, 1), 0).astype(F32) * (slope * (-float(d)))

        staged = d > STAGE_STRIDE
        if staged:
            assert d % STAGE_STRIDE == 0 and d // STAGE_STRIDE <= STAGE_STRIDE
            quarter = S // STAGE_STRIDE
            for t_idx, ref in enumerate((q_ref, k_ref, v_ref)):
                for r1 in range(STAGE_STRIDE):
                    stage_scr[t_idx, r1 * quarter:(r1 + 1) * quarter, :] = ref[0, pl.ds(r1, quarter, stride=STAGE_STRIDE), :]

        def out_rows(r, i, d=d):
            start = r + i * (d * blk)
            return pl.ds(start, blk, stride=d) if d > 1 else pl.ds(start, blk)

        def staged_rows(r, i, d=d):
            s2 = d // STAGE_STRIDE
            r2, r1 = _divmod_pow2(r, STAGE_STRIDE)
            return pl.ds(r1 * (S // STAGE_STRIDE) + r2 + i * (s2 * blk), blk, stride=s2)

        def load(t_idx, ref, r, i, staged=staged):
            return stage_scr[t_idx, staged_rows(r, i), :] if staged else ref[0, out_rows(r, i), :]

        def body(n0, carry, gi=gi, d=d, has_prev=has_prev, q_ref=q_ref, k_ref=k_ref, v_ref=v_ref,
                 bias=bias, bias_first=bias_first, row_term=row_term, out_rows=out_rows, load=load,
                 staged=staged, staged_rows=staged_rows):
            ri = [_divmod_pow2(n0 * DIL_GROUP_BLOCKS + g, d)[::-1] for g in range(DIL_GROUP_BLOCKS)]
            qs = [load(0, q_ref, r, i).astype(BF) for r, i in ri]
            if has_prev:
                ips = [jnp.maximum(i - 1, 0) for _, i in ri]
                ks = [jnp.concatenate([load(1, k_ref, r, ip), load(1, k_ref, r, i)], axis=0).astype(BF)
                      for (r, i), ip in zip(ri, ips)]
                vs = [jnp.concatenate([load(2, v_ref, r, ip), load(2, v_ref, r, i)], axis=0).astype(BF)
                      for (r, i), ip in zip(ri, ips)]
                bs = [jnp.where(i > 0, bias, bias_first) for _, i in ri]
            else:
                ks = [load(1, k_ref, r, i).astype(BF) for r, i in ri]
                vs = [load(2, v_ref, r, i).astype(BF) for r, i in ri]
                bs = [bias] * len(ri)
            ss = [lax.dot_general(q, k, NT, preferred_element_type=F32) + b for q, k, b in zip(qs, ks, bs)]
            ms = [jnp.max(s, axis=-1, keepdims=True) for s in ss]
            ps = [jnp.exp2(s - m) for s, m in zip(ss, ms)]
            ls = [jnp.sum(p, axis=-1, keepdims=True) for p in ps]
            os_ = [jnp.dot(p.astype(BF), v, preferred_element_type=F32) * (1.0 / l) for p, v, l in zip(ps, vs, ls)]
            for (r, i), o, m, l in zip(ri, os_, ms, ls):
                lse_b = jnp.broadcast_to((m + jnp.log2(l)) * LN2 + row_term, (blk, HEAD_DIM))
                if staged:
                    rows = staged_rows(r, i)
                    ostage_scr[0, rows, :] = o
                    ostage_scr[1, rows, :] = lse_b
                else:
                    rows = out_rows(r, i)
                    o_scr[gi, rows, :] = o
                    l_scr[gi, rows, :] = lse_b
            return carry

        assert (d * nqb) % DIL_GROUP_BLOCKS == 0
        lax.fori_loop(0, d * nqb // DIL_GROUP_BLOCKS, body, 0)
        if staged:
            for r1 in range(STAGE_STRIDE):
                dst = pl.ds(r1, quarter, stride=STAGE_STRIDE)
                o_scr[gi, dst, :] = ostage_scr[0, r1 * quarter:(r1 + 1) * quarter, :]
                l_scr[gi, dst, :] = ostage_scr[1, r1 * quarter:(r1 + 1) * quarter, :]

    tr = 256
    for c in range(S // tr):
        rows = slice(c * tr, (c + 1) * tr)
        ls = [l_scr[gi, rows, :] for gi in range(ng)]
        m = functools.reduce(jnp.maximum, ls)
        es = [jnp.exp(x - m) for x in ls]
        num = sum(e * o_scr[gi, rows, :] for gi, e in enumerate(es))
        o_ref[0, rows, :] = (num * (1.0 / sum(es))).astype(o_ref.dtype)


def dilated_attention(q, k, v):
    B, S, _ = q.shape
    ng, Hg = len(DIL_GROUPS), A_HEADS_PER_GROUP
    for w, d in DIL_GROUPS:
        assert w // d <= A_BLOCK and S % (d * A_BLOCK) == 0
    specs = [pl.BlockSpec((1, S, HEAD_DIM), lambda b, j, gi=gi: (b, 0, gi * Hg + j)) for gi in range(ng)]
    return pl.pallas_call(
        _dil_kernel,
        grid=(B, Hg),
        in_specs=specs * 3,
        out_specs=pl.BlockSpec((1, S, HEAD_DIM), lambda b, j: (b, 0, j)),
        out_shape=jax.ShapeDtypeStruct((B, S, Hg * HEAD_DIM), BF),
        scratch_shapes=[pltpu.VMEM((ng, S, HEAD_DIM), F32), pltpu.VMEM((ng, S, HEAD_DIM), F32),
                        pltpu.VMEM((3, S, HEAD_DIM), F32), pltpu.VMEM((2, S, HEAD_DIM), F32)],
        compiler_params=_params("arbitrary", "arbitrary"),
        name="dilated_attention",
    )(*([q] * ng + [k] * ng + [v] * ng))


def _mem_attn_kernel(q_ref, kv_ref, o_ref):
    for h in range(MEM_HEADS):
        c = slice(h * HEAD_DIM, (h + 1) * HEAD_DIM)
        k = kv_ref[0, :, c]
        v = kv_ref[0, :, MEM_HEADS * HEAD_DIM + h * HEAD_DIM:MEM_HEADS * HEAD_DIM + (h + 1) * HEAD_DIM]
        s = lax.dot_general(q_ref[0, :, c], k, NT, preferred_element_type=F32) * SCALE
        m = jnp.max(s, axis=-1, keepdims=True)
        e = jnp.exp(s - m)
        p = e * (1.0 / jnp.sum(e, axis=-1, keepdims=True))
        o_ref[0, :, c] = jnp.dot(p.astype(BF), v, preferred_element_type=F32).astype(o_ref.dtype)


def memory_attention(q, q_col0, kv, layer, tq=1024):
    B, S, _ = q.shape
    w = MEM_HEADS * HEAD_DIM
    cb = q_col0 // w
    return pl.pallas_call(
        _mem_attn_kernel,
        grid=(B, S // tq),
        in_specs=[pl.BlockSpec((1, tq, w), lambda b, i: (b, i, cb)),
                  pl.BlockSpec((None, 1, N_MEM, 2 * w), lambda b, i: (layer, b, 0, 0))],
        out_specs=pl.BlockSpec((1, tq, w), lambda b, i: (b, i, 0)),
        out_shape=jax.ShapeDtypeStruct((B, S, w), BF),
        compiler_params=_params("arbitrary", "arbitrary"),
        name="memory_attention",
    )(q, kv)


def _gelu_tanh(x):
    return 0.5 * x * (1.0 + jnp.tanh(math.sqrt(2.0 / math.pi) * (x + 0.044715 * (x * x * x))))


def _compress_kernel(out_scale, x_ref, w1_ref, w2_ref, pe_ref, o_ref, w1b_ref):
    @pl.when((pl.program_id(0) == 0) & (pl.program_id(1) == 0))
    def _():
        w1b_ref[...] = w1_ref[...].astype(BF)

    nchunk = o_ref.shape[1]
    top = jnp.zeros((nchunk, CMP_HIDDEN), F32)
    bot = jnp.zeros((nchunk, CMP_HIDDEN), F32)
    for l in range(CMP_STRIDE):
        xl = x_ref[0, pl.ds(l, nchunk, stride=CMP_STRIDE), :]
        lo, hi = l, CMP_STRIDE + l
        top += jnp.dot((xl + pe_ref[lo:lo + 1, :]).astype(BF),
                       w1b_ref[lo * HEAD_DIM:(lo + 1) * HEAD_DIM, :], preferred_element_type=F32)
        bot += jnp.dot((xl + pe_ref[hi:hi + 1, :]).astype(BF),
                       w1b_ref[hi * HEAD_DIM:(hi + 1) * HEAD_DIM, :], preferred_element_type=F32)
    hid = _gelu_tanh(top + pltpu.roll(bot, nchunk - 1, 0))
    out = jnp.dot(hid.astype(BF), w2_ref[...].astype(BF), preferred_element_type=F32)
    o_ref[0] = (out * out_scale).astype(o_ref.dtype)


def compress_blocks(kvc, sec, w1, w2, pe, out_scale=1.0):
    B, S, C = kvc.shape
    assert CMP_LEN == 2 * CMP_STRIDE
    G = B_KV_GROUPS
    nchunk = S // CMP_STRIDE
    return pl.pallas_call(
        functools.partial(_compress_kernel, out_scale),
        grid=(B, G),
        in_specs=[pl.BlockSpec((1, S, HEAD_DIM), lambda b, g: (b, 0, sec * G + g)),
                  pl.BlockSpec(w1.shape, lambda b, g: (0, 0)),
                  pl.BlockSpec(w2.shape, lambda b, g: (0, 0)),
                  pl.BlockSpec(pe.shape, lambda b, g: (0, 0))],
        out_specs=pl.BlockSpec((1, nchunk, HEAD_DIM), lambda b, g: (b, 0, g)),
        out_shape=jax.ShapeDtypeStruct((B, nchunk, G * HEAD_DIM), BF),
        scratch_shapes=[pltpu.VMEM(w1.shape, BF)],
        compiler_params=_params("arbitrary", "arbitrary"),
        name=f"compress_{sec}",
    )(kvc, w1, w2, pe)


ONES_ROWS = 16
AUG_K = LANES
NEG_ROWS = 16


def _key_aug(nk, bps):
    idx = _iota((nk, AUG_K), 0)
    lane = _iota((nk, AUG_K), 1)
    piece = jnp.where((lane - NEG_ROWS) % 2 == 0, (idx // 256) * 256, idx % 256)
    val = jnp.where(lane < bps, (idx // SLC_LEN == lane).astype(jnp.int32),
                    jnp.where((lane >= NEG_ROWS) & (lane < NEG_ROWS + 6), piece, 0))
    return val.astype(F32).astype(BF)


def _offset_masks(n, nk, tq, lo, hi):
    dist = (_iota((n, nk, tq), 0) * tq + _iota((n, nk, tq), 2)) - _iota((n, nk, tq), 1)
    return jnp.where((dist >= lo) & (dist <= hi), 0.0, NEG).astype(F32)


def _t_bf(x):
    return x.astype(F32).T.astype(BF)


def _nsa_kernel(tq, q_ref, gate_ref, kc_ref, vc_ref, ks_ref, vs_ref, kw_ref, vw_ref,
                sl_ref, kaug_w_ref, kaug_s_ref, wmask_ref, cmask_ref, o_ref,
                vst_ref, vwt_ref, qt_ref, ocmp_ref, neg_ref, raw_a_ref, raw_b_ref, pc_ref):
    S = ks_ref.shape[1]
    ncmp = kc_ref.shape[1]
    nsel = S // SLC_LEN
    i = pl.program_id(2)

    t0 = i * tq
    tw = B_REP * tq
    nt = S // tq
    n_back = -(-(WIN_LEN - 1) // tq)
    nwin = (n_back + 1) * tq
    kcs = vst_ref.shape[2]
    bps = kcs // SLC_LEN

    def per_head(fn, n=B_REP):
        return jnp.concatenate([fn(r) for r in range(n)], axis=1)

    def bf_exact(x):
        return x.astype(BF).astype(F32)

    @pl.when(i == 0)
    def _():
        for src_ref, dst_ref in ((vs_ref, vst_ref), (vw_ref, vwt_ref)):
            n, _, w = dst_ref.shape
            ones = jnp.ones((ONES_ROWS, w), BF)
            for c in range(n):
                dst_ref[c] = jnp.concatenate([_t_bf(src_ref[0, c * w:(c + 1) * w, :]), ones], axis=0)
        vct = _t_bf(vc_ref[0])
        for t in range(nt):
            for r in range(B_REP):
                qt_ref[t, r] = _t_bf(q_ref[0, t * tq:(t + 1) * tq, r * HEAD_DIM:(r + 1) * HEAD_DIM])
        q_w = jnp.concatenate([qt_ref[t, r] for r in range(B_REP) for t in range(nt)], axis=1)
        sl2_w = per_head(lambda r: jnp.broadcast_to(sl_ref[0, 0:1, r:r + 1], (1, S))) * LOG2E
        pos = _iota((1, S), 1)
        pos_w = per_head(lambda r: pos)

        dist_c = pos_w - (_iota((ncmp, 1), 0) * CMP_STRIDE + (CMP_LEN - 1))
        s = (jnp.dot(kc_ref[0], q_w, preferred_element_type=F32)
             + jnp.where(dist_c >= 0, dist_c.astype(F32) * (-sl2_w), NEG))
        e = jnp.exp2(s - jnp.max(s, axis=0, keepdims=True))
        has_block = (pos_w >= CMP_LEN - 1).astype(F32)
        p = e * (has_block / jnp.maximum(jnp.sum(e, axis=0, keepdims=True), 1e-30))
        o_cmp_w = jnp.dot(vct, p.astype(BF), preferred_element_type=F32)
        for t in range(nt):
            for r in range(B_REP):
                ocmp_ref[t, r] = o_cmp_w[:, r * S + t * tq:r * S + (t + 1) * tq]

        ratio = SLC_LEN // CMP_STRIDE
        assert SLC_LEN % CMP_STRIDE == 0 and CMP_LEN <= SLC_LEN and ncmp >= nsel * ratio
        p_heads = sum(p[:, r * S:(r + 1) * S] for r in range(B_REP))
        for lt in range(S // LANES):
            pc_ref[lt] = p_heads[:, lt * LANES:(lt + 1) * LANES]
        imp = jnp.zeros((nsel, S), F32)
        for o in range(1 - (-(-CMP_LEN // CMP_STRIDE)), ratio):
            overlap = max(min(o * CMP_STRIDE + CMP_LEN, SLC_LEN) - max(o * CMP_STRIDE, 0), 0)
            if overlap == 0:
                continue
            part = jnp.concatenate([pc_ref[lt, pl.ds(o % ratio, nsel, stride=ratio), :]
                                    for lt in range(S // LANES)], axis=1)
            if o < 0:
                part = jnp.where(_iota((nsel, S), 0) == 0, 0.0, pltpu.roll(part, 1, 0))
            imp = imp + (overlap / CMP_LEN) * part

        jb = _iota((nsel, S), 0)
        cur = pos // SLC_LEN
        forced = (jb == 0) | (jb == cur) | (jb == cur - 1)
        imp = jnp.where(forced, SEL_FORCE, jnp.where(jb > cur, -SEL_FORCE, imp))
        beaten_by = jnp.zeros((nsel, S), jnp.int32)
        for j in range(nsel):
            row = imp[j:j + 1, :]
            beats = (row > imp) | ((row == imp) & (jb > j))
            beaten_by = beaten_by + beats.astype(jnp.int32)
        neg_w = jnp.where(beaten_by >= min(SLC_TOPK, nsel), NEG, 0.0)
        for t in range(nt):
            neg_ref[t] = neg_w[:, t * tq:(t + 1) * tq]

    q_all = per_head(lambda r: qt_ref[i, r])
    o_cmp = per_head(lambda r: ocmp_ref[i, r])
    sl_row = per_head(lambda r: jnp.broadcast_to(sl_ref[0, 0:1, r:r + 1], (1, tq)))
    sl2_row = sl_row * LOG2E

    c1 = bf_exact(sl2_row)
    c2 = bf_exact(sl2_row - c1)
    c3 = bf_exact(sl2_row - c1 - c2)
    slope_rows = jnp.concatenate([c1, c1, c2, c2, c3, c3, jnp.zeros((AUG_K - NEG_ROWS - 6, tw), F32)],
                                 axis=0).astype(BF)

    def scores(k_c, k_aug, neg_rows):
        q_full = jnp.concatenate([q_all, neg_rows, slope_rows], axis=0)
        return jnp.dot(jnp.concatenate([k_c, k_aug], axis=1), q_full, preferred_element_type=F32)

    def neg_rows_of(n):
        rows = neg_ref[i, pl.ds(pl.multiple_of(n * bps, bps), bps), :]
        return jnp.concatenate([per_head(lambda r: rows), jnp.zeros((NEG_ROWS - bps, tw), F32)], axis=0).astype(BF)

    no_rows = jnp.zeros((NEG_ROWS, tw), BF)
    c0 = jnp.maximum(i - n_back, 0)
    wbase = pl.multiple_of(c0 * tq, tq)
    raw_w = scores(kw_ref[0, pl.ds(wbase, nwin), :], kaug_w_ref[...], no_rows)
    k_aug = kaug_s_ref[...]
    n_last = (t0 + tq - 1) // kcs
    raw_d = scores(ks_ref[0, pl.ds(pl.multiple_of(n_last * kcs, kcs), kcs), :], k_aug, no_rows)

    def softmax_step(raw, vt_c, mask, c_off, st):
        m, acc = st
        s = raw if mask is None else raw + per_head(lambda r: mask)
        m_new = jnp.maximum(m, jnp.max(s, axis=0, keepdims=True) + c_off)
        p = jnp.exp2(s - (m_new - c_off))
        acc = jnp.exp2(m - m_new) * acc + jnp.dot(vt_c, p.astype(BF), preferred_element_type=F32)
        return m_new, acc

    def finish(st):
        _, acc = st
        return acc[:HEAD_DIM] * (1.0 / acc[HEAD_DIM:HEAD_DIM + 1])

    init = (jnp.full((1, tw), NEG, F32), jnp.zeros((HEAD_DIM + ONES_ROWS, tw), F32))
    no_off = jnp.zeros((1, 1), F32)

    vwt = jnp.concatenate([vwt_ref[c0 + j] for j in range(n_back + 1)], axis=1)
    o_win = finish(softmax_step(raw_w, vwt, wmask_ref[i - c0], no_off, init))

    def slab_scores(n):
        return scores(ks_ref[0, pl.ds(pl.multiple_of(n * kcs, kcs), kcs), :], k_aug, neg_rows_of(n))

    raw_a_ref[...] = slab_scores(0)
    sel_rows = [jnp.broadcast_to(neg_ref[i, pl.ds(n_last * bps + b, 1), :], (SLC_LEN, tq)) for b in range(bps)]
    mask_d = jnp.concatenate(sel_rows, axis=0) + cmask_ref[i - n_last * (kcs // tq)]
    st = softmax_step(raw_d, vst_ref[n_last], mask_d, (n_last * kcs - t0).astype(F32) * sl2_row, init)

    def sel_slab(n, st, src_ref, dst_ref):
        dst_ref[...] = slab_scores(jnp.minimum(n + 1, n_last))
        return softmax_step(src_ref[...], vst_ref[n], None, (n * kcs - t0).astype(F32) * sl2_row, st)

    def sel_pair(j, st):
        st = sel_slab(2 * j, st, raw_a_ref, raw_b_ref)
        return lax.cond(2 * j + 1 < n_last, lambda s: sel_slab(2 * j + 1, s, raw_b_ref, raw_a_ref),
                        lambda s: s, st)

    o_slc = finish(lax.fori_loop(0, (n_last + 1) // 2, sel_pair, st))

    gate_t = (1.0 / (1.0 + jnp.exp(-gate_ref[0]))).T
    for r in range(B_REP):
        c = slice(r * tq, (r + 1) * tq)
        o_t = (gate_t[3 * r:3 * r + 1] * o_cmp[:, c] + gate_t[3 * r + 1:3 * r + 2] * o_slc[:, c]
               + gate_t[3 * r + 2:3 * r + 3] * o_win[:, c])
        o_ref[0, :, r * HEAD_DIM:(r + 1) * HEAD_DIM] = o_t.T.astype(o_ref.dtype)


def nsa_attention(qp, gates, kc, vc, kv, tq=256):
    B, S, _ = qp.shape
    G = B_KV_GROUPS
    gq = B_REP * HEAD_DIM
    ncmp = kc.shape[1]
    kcs = 2 * tq
    nsel = S // SLC_LEN
    assert kcs % SLC_LEN == 0 and S % kcs == 0 and S >= (-(-(WIN_LEN - 1) // tq) + 1) * tq
    slopes = 2.0 ** (-8.0 * jnp.arange(1, B_HEADS + 1, dtype=F32) / B_HEADS)
    sl = jnp.zeros((G, 8, LANES), F32).at[:, :, :B_REP].set(
        jnp.broadcast_to(slopes.reshape(G, 1, B_REP), (G, 8, B_REP)))
    kvspec = lambda sec: pl.BlockSpec((1, S, HEAD_DIM), lambda b, g, i: (b, 0, sec * G + g))
    cspec = pl.BlockSpec((1, ncmp, HEAD_DIM), lambda b, g, i: (b, 0, g))
    vt_rows = HEAD_DIM + ONES_ROWS
    n_back = -(-(WIN_LEN - 1) // tq)
    nwin = (n_back + 1) * tq
    bps = kcs // SLC_LEN
    assert bps <= NEG_ROWS and NEG_ROWS + 6 <= AUG_K
    consts = [_key_aug(nwin, bps), _key_aug(kcs, bps), _offset_masks(n_back + 1, nwin, tq, 0, WIN_LEN - 1),
              _offset_masks(kcs // tq, kcs, tq, 0, S)]
    whole = lambda a: pl.BlockSpec(a.shape, lambda b, g, i, nd=a.ndim: (0,) * nd)
    return pl.pallas_call(
        functools.partial(_nsa_kernel, tq),
        grid=(B, G, S // tq),
        in_specs=[pl.BlockSpec((1, S, gq), lambda b, g, i: (b, 0, g)),
                  pl.BlockSpec((1, tq, LANES), lambda b, g, i: (b, i, g)),
                  cspec, cspec, kvspec(0), kvspec(1), kvspec(2), kvspec(3),
                  pl.BlockSpec((1, 8, LANES), lambda b, g, i: (g, 0, 0))] + [whole(a) for a in consts],
        out_specs=pl.BlockSpec((1, tq, gq), lambda b, g, i: (b, i, g)),
        out_shape=jax.ShapeDtypeStruct((B, S, B_HEADS * HEAD_DIM), BF),
        scratch_shapes=[pltpu.VMEM((S // kcs, vt_rows, kcs), BF), pltpu.VMEM((S // tq, vt_rows, tq), BF),
                        pltpu.VMEM((S // tq, B_REP, HEAD_DIM, tq), BF), pltpu.VMEM((S // tq, B_REP, HEAD_DIM, tq), F32),
                        pltpu.VMEM((S // tq, nsel, tq), F32),
                        pltpu.VMEM((kcs, B_REP * tq), F32), pltpu.VMEM((kcs, B_REP * tq), F32),
                        pltpu.VMEM((S // LANES, ncmp, LANES), F32)],
        compiler_params=_params("arbitrary", "arbitrary", "arbitrary"),
        name="nsa_attention",
    )(qp, gates, kc, vc, kv, kv, kv, kv, sl, *consts)


def kernel(x, mem, norm_g, a_w_in, a_w_out, b_w_in, b_w_out, mem_w_kv, ffn_w_gu, ffn_w_down,
           kv_norm_g, kv_w, cmp_pe, cmp_wk1, cmp_wk2, cmp_wv1, cmp_wv2):
    B, S, D = x.shape
    M = B * S
    n_a = DEPTH // 2
    qa_cols = len(DIL_GROUPS) * A_HEADS_PER_GROUP * HEAD_DIM
    qb_cols = B_HEADS * HEAD_DIM
    mq_cols = MEM_HEADS * HEAD_DIM
    n_gate = 3 * B_HEADS
    G = B_KV_GROUPS

    b_w_q = b_w_in[:, :, :qb_cols]
    b_w_mq = b_w_in[:, :, qb_cols + n_gate:]
    b_w_gate = jnp.pad(b_w_in[:, :, qb_cols:qb_cols + n_gate].reshape(-1, D, G, n_gate // G),
                       ((0, 0), (0, 0), (0, 0), (0, LANES - n_gate // G))).reshape(-1, D, G * LANES)
    kv_w3 = kv_w[None]

    x2 = x.reshape(M, D)
    mkv = memory_kv_all_layers(mem.reshape(B * N_MEM, D), norm_g[:, 4], mem_w_kv).reshape(
        DEPTH, B, N_MEM, 2 * mq_cols)
    u = rmsnorm_rows(x2, norm_g[0, 0])
    shared = None
    for l in range(DEPTH):
        g = norm_g[l]
        if l < n_a:
            q, w_out = matmul(u, a_w_in, l, col0=0, ncols=qa_cols, out_dtype=F32, cast=(a_w_out, l), name="a_q")
            k = matmul(u, a_w_in, l, col0=qa_cols, ncols=qa_cols, out_dtype=F32,
                       col_scale=jnp.full((qa_cols,), SCALE2, F32), name="a_k")
            v = matmul(u, a_w_in, l, col0=2 * qa_cols, ncols=qa_cols, out_dtype=F32, name="a_v")
            mq = matmul(u, a_w_in, l, col0=3 * qa_cols, ncols=mq_cols, name="a_mq").reshape(B, S, mq_cols)
            o_main = dilated_attention(*(t.reshape(B, S, qa_cols) for t in (q, k, v)))
            o_mem = memory_attention(mq, 0, mkv, l)
        else:
            li = l - n_a
            if shared is None:
                kvc = matmul(kvn, kv_w3, 0, col0=0, ncols=2 * G * HEAD_DIM, out_dtype=F32,
                             name="kv_cmp").reshape(B, S, 2 * G * HEAD_DIM)
                sec = G * HEAD_DIM
                key_scale = jnp.tile(jnp.concatenate([jnp.full((sec,), SCALE2, F32), jnp.ones((sec,), F32)]), 2)
                kvr = matmul(kvn, kv_w3, 0, col0=2 * G * HEAD_DIM, ncols=4 * G * HEAD_DIM, col_scale=key_scale,
                             name="kv_rest").reshape(B, S, 4 * G * HEAD_DIM)
                kc = compress_blocks(kvc, 0, cmp_wk1, cmp_wk2, cmp_pe[0], out_scale=SCALE2)
                vc = compress_blocks(kvc, 1, cmp_wv1, cmp_wv2, cmp_pe[1])
                shared = (kc, vc, kvr)
            qb = matmul(u, b_w_q, li, name="b_q").reshape(B, S, qb_cols)
            mq = matmul(u, b_w_mq, li, name="b_mq").reshape(B, S, mq_cols)
            gates, w_out = matmul(u, b_w_gate, li, out_dtype=F32, cast=(b_w_out, li), name="b_gate")
            o_main = nsa_attention(qb, gates.reshape(B, S, G * LANES), *shared)
            o_mem = memory_attention(mq, 0, mkv, l)
        x2, (u,) = proj_residual_norm(o_main.reshape(M, -1), o_mem.reshape(M, mq_cols), 0, mq_cols,
                                      w_out[None], 0, x2, g[1], g[2:3])
        h, w_down = ffn_gate_up(u, ffn_w_gu, l, ffn_w_down)
        next_gains = [norm_g[l + 1, 0]] if l + 1 < DEPTH else []
        if l + 1 == n_a:
            next_gains.append(kv_norm_g)
        x2, us = proj_residual_norm(h, None, 0, 0, w_down[None], 0, x2, g[3],
                                    jnp.stack(next_gains) if next_gains else None, tm=256, n_sub=2)
        u = us[0] if us else None
        if l + 1 == n_a:
            kvn = us[1]
    return x2.reshape(B, S, D)
```
